```python
import math
import numpy as np
import jax
import jax.numpy as jnp
from jax import lax

D_MODEL = 1024
BATCH = 32
SEQ = 2048
DEPTH = 2

HEAD_DIM = 64
H_MLA = 6
H_NA = 5
H_DIFF = 5
W_MLA = H_MLA * HEAD_DIM
W_NA = H_NA * HEAD_DIM
W_DIFF = H_DIFF * HEAD_DIM
D_MIX = W_MLA + W_NA + W_DIFF
Q_LORA = 256
KV_LORA = 128
MLA_NOPE = 64
MLA_ROPE = 32
MLA_V = HEAD_DIM
MLA_QK = MLA_NOPE + MLA_ROPE
ROPE_BASE = 10000.0
GRID_W = 64
NA_KR_MAX = 8
NA_KC = 16
DIFF_D = 32
DIFF_V = 2 * DIFF_D
T5_BUCKETS = 32
T5_MAX_DIST = 128
D_FF = 2816
Q_BLOCK = 128
EPS = 1e-6
IN_WIDTHS = (Q_LORA, KV_LORA, MLA_ROPE, W_NA, W_NA, W_NA,
             H_DIFF * 2 * DIFF_D, H_DIFF * 2 * DIFF_D, H_DIFF * DIFF_V)
N_IN = sum(IN_WIDTHS)

kernel_name = "hybrid_mla_natten_diffattn_macaron"


def rms_norm(x, g):
    xf = x.astype(jnp.float32)
    y = xf * lax.rsqrt(jnp.mean(xf * xf, axis=-1, keepdims=True) + EPS)
    return (y * g.astype(jnp.float32)).astype(x.dtype)


def swiglu(x, w_gate, w_up, w_down):
    return (jax.nn.silu(x @ w_gate) * (x @ w_up)) @ w_down


def rope_tables(seq, dim):
    inv = ROPE_BASE ** (-jnp.arange(0, dim, 2, dtype=jnp.float32) / dim)
    ang = jnp.arange(seq, dtype=jnp.float32)[:, None] * inv[None, :]
    return jnp.cos(ang), jnp.sin(ang)


def apply_rope(x, cos, sin):
    x1, x2 = jnp.split(x.astype(jnp.float32), 2, axis=-1)
    c, s = cos[:, None, :], sin[:, None, :]
    return jnp.concatenate([x1 * c - x2 * s, x1 * s + x2 * c], axis=-1).astype(x.dtype)


def t5_bucket(rel):
    half = T5_BUCKETS // 2
    max_exact = half // 2
    n = jnp.abs(rel)
    nf = jnp.maximum(n, 1).astype(jnp.float32)
    large = max_exact + (jnp.log(nf / max_exact) / math.log(T5_MAX_DIST / max_exact)
                         * (half - max_exact)).astype(jnp.int32)
    large = jnp.minimum(large, half - 1)
    return jnp.where(rel > 0, half, 0) + jnp.where(n < max_exact, n, large)


def _to_blocks(t):
    b, s, h, d = t.shape
    return t.reshape(b, s // Q_BLOCK, Q_BLOCK, h, d).transpose(1, 0, 3, 2, 4)


def _from_blocks(o):
    nb, b, h, qb, d = o.shape
    return o.transpose(1, 0, 3, 2, 4).reshape(b, nb * qb, h * d)


def dense_softmax_attention(q, k, v):
    scale = q.shape[-1] ** -0.5
    kt = k.transpose(0, 2, 1, 3)
    vt = v.transpose(0, 2, 1, 3)

    def block(qb):
        s = jnp.einsum('bhqd,bhkd->bhqk', qb, kt).astype(jnp.float32) * scale
        p = jax.nn.softmax(s, axis=-1).astype(v.dtype)
        return jnp.einsum('bhqk,bhkd->bhqd', p, vt)

    return _from_blocks(lax.map(block, _to_blocks(q)))


def diff_attention(q1, q2, k1, k2, v, lam, t5_table):
    S = q1.shape[1]
    scale = q1.shape[-1] ** -0.5
    k1t, k2t = k1.transpose(0, 2, 1, 3), k2.transpose(0, 2, 1, 3)
    vt = v.transpose(0, 2, 1, 3)
    kpos = jnp.arange(S, dtype=jnp.int32)

    def block(args):
        q1b, q2b, start = args
        qpos = start + jnp.arange(Q_BLOCK, dtype=jnp.int32)
        bias = t5_table[t5_bucket(kpos[None, :] - qpos[:, None])]
        bias = bias.astype(jnp.float32).transpose(2, 0, 1)[None]
        s1 = jnp.einsum('bhqd,bhkd->bhqk', q1b, k1t).astype(jnp.float32) * scale + bias
        s2 = jnp.einsum('bhqd,bhkd->bhqk', q2b, k2t).astype(jnp.float32) * scale + bias
        a = jax.nn.softmax(s1, axis=-1) - lam * jax.nn.softmax(s2, axis=-1)
        return jnp.einsum('bhqk,bhkd->bhqd', a.astype(v.dtype), vt)

    starts = jnp.arange(S // Q_BLOCK, dtype=jnp.int32) * Q_BLOCK
    return _from_blocks(lax.map(block, (_to_blocks(q1), _to_blocks(q2), starts)))


def neighbourhood_attention(q, k, v, rpb):
    B, S, H, d = q.shape
    rows = S // GRID_W
    kr = min(NA_KR_MAX, rows)
    scale = d ** -0.5

    def grid(t):
        return t.reshape(B, rows, GRID_W, H, d).transpose(0, 3, 1, 2, 4)

    qg, kg, vg = grid(q), grid(k), grid(v)
    cols = np.arange(GRID_W)
    col_start = np.clip(cols - NA_KC // 2, 0, GRID_W - NA_KC)
    col_idx = col_start[:, None] + np.arange(NA_KC)[None, :]
    col_off = col_idx - cols[:, None] + (NA_KC - 1)
    r_idx = np.arange(rows)
    row_start = np.clip(r_idx - kr // 2, 0, rows - kr)
    rpb_cols = rpb[:, :, col_off]

    def row(args):
        q_row, r, rs = args
        kb = lax.dynamic_slice_in_dim(kg, rs, kr, axis=2)
        vb = lax.dynamic_slice_in_dim(vg, rs, kr, axis=2)
        kw = jnp.take(kb, col_idx, axis=3)
        vw = jnp.take(vb, col_idx, axis=3)
        row_off = rs + jnp.arange(kr, dtype=jnp.int32) - r + (NA_KR_MAX - 1)
        bias = jnp.take(rpb_cols, row_off, axis=1).transpose(0, 2, 1, 3)
        s = jnp.einsum('bhcd,bhicjd->bhcij', q_row, kw).astype(jnp.float32) * scale
        s = s + bias.astype(jnp.float32)[None]
        p = jax.nn.softmax(s.reshape(B, H, GRID_W, kr * NA_KC), axis=-1)
        p = p.reshape(B, H, GRID_W, kr, NA_KC).astype(v.dtype)
        return jnp.einsum('bhcij,bhicjd->bhcd', p, vw)

    out = lax.map(row, (qg.transpose(2, 0, 1, 3, 4),
                        jnp.asarray(r_idx, dtype=jnp.int32),
                        jnp.asarray(row_start, dtype=jnp.int32)))
    return out.transpose(1, 0, 3, 2, 4).reshape(B, S, H * d)


def setup_inputs(seed: int = 0) -> dict:
    key = jax.random.key(seed)
    ks = iter(jax.random.split(key, 40))

    def nrm(shape, scale):
        return jax.random.normal(next(ks), shape, jnp.float32) * scale

    def gain(shape):
        return 1.0 + 0.01 * jax.random.normal(next(ks), shape, jnp.float32)

    L = DEPTH
    return {
        "x": nrm((BATCH, SEQ, D_MODEL), 1.0),
        "ffn1_norm": gain((L, D_MODEL)),
        "ffn1_w_gate": nrm((L, D_MODEL, D_FF), D_MODEL ** -0.5),
        "ffn1_w_up": nrm((L, D_MODEL, D_FF), D_MODEL ** -0.5),
        "ffn1_w_down": nrm((L, D_FF, D_MODEL), D_FF ** -0.5),
        "mix_norm": gain((L, D_MODEL)),
        "w_in": nrm((L, D_MODEL, N_IN), D_MODEL ** -0.5),
        "mla_q_norm": gain((L, Q_LORA)),
        "mla_w_uq": nrm((L, Q_LORA, H_MLA * MLA_QK), Q_LORA ** -0.5),
        "mla_kv_norm": gain((L, KV_LORA)),
        "mla_w_ukv": nrm((L, KV_LORA, H_MLA * (MLA_NOPE + MLA_V)), KV_LORA ** -0.5),
        "mla_qk_q_norm": gain((L, MLA_QK)),
        "mla_qk_k_norm": gain((L, MLA_QK)),
        "mla_out_norm": gain((L, W_MLA)),
        "na_qk_q_norm": gain((L, HEAD_DIM)),
        "na_qk_k_norm": gain((L, HEAD_DIM)),
        "na_rpb": nrm((L, H_NA, 2 * NA_KR_MAX - 1, 2 * NA_KC - 1), 0.1),
        "na_out_norm": gain((L, W_NA)),
        "diff_qk_q_norm": gain((L, DIFF_D)),
        "diff_qk_k_norm": gain((L, DIFF_D)),
        "diff_lambda_q1": nrm((L, DIFF_D), 0.1),
        "diff_lambda_k1": nrm((L, DIFF_D), 0.1),
        "diff_lambda_q2": nrm((L, DIFF_D), 0.1),
        "diff_lambda_k2": nrm((L, DIFF_D), 0.1),
        "diff_subln": gain((L, DIFF_V)),
        "t5_table": nrm((T5_BUCKETS, H_DIFF), 0.1),
        "w_out": nrm((L, D_MIX, D_MODEL), D_MIX ** -0.5),
        "ffn2_norm": gain((L, D_MODEL)),
        "ffn2_w_gate": nrm((L, D_MODEL, D_FF), D_MODEL ** -0.5),
        "ffn2_w_up": nrm((L, D_MODEL, D_FF), D_MODEL ** -0.5),
        "ffn2_w_down": nrm((L, D_FF, D_MODEL), D_FF ** -0.5),
    }


def reference(x, ffn1_norm, ffn1_w_gate, ffn1_w_up, ffn1_w_down, mix_norm, w_in,
              mla_q_norm, mla_w_uq, mla_kv_norm, mla_w_ukv, mla_qk_q_norm, mla_qk_k_norm,
              mla_out_norm, na_qk_q_norm, na_qk_k_norm, na_rpb, na_out_norm,
              diff_qk_q_norm, diff_qk_k_norm, diff_lambda_q1, diff_lambda_k1,
              diff_lambda_q2, diff_lambda_k2, diff_subln, t5_table, w_out,
              ffn2_norm, ffn2_w_gate, ffn2_w_up, ffn2_w_down):
    B, S, _ = x.shape
    cos, sin = rope_tables(S, MLA_ROPE)
    split_at = [int(c) for c in np.cumsum(IN_WIDTHS)[:-1]]
    h = x
    for l in range(DEPTH):
        h = h + 0.5 * swiglu(rms_norm(h, ffn1_norm[l]), ffn1_w_gate[l], ffn1_w_up[l], ffn1_w_down[l])

        u = rms_norm(h, mix_norm[l]) @ w_in[l]
        c_q, c_kv, k_rope, na_q, na_k, na_v, df_q, df_k, df_v = jnp.split(u, split_at, axis=-1)

        q = (rms_norm(c_q, mla_q_norm[l]) @ mla_w_uq[l]).reshape(B, S, H_MLA, MLA_QK)
        kv = (rms_norm(c_kv, mla_kv_norm[l]) @ mla_w_ukv[l]).reshape(B, S, H_MLA, MLA_NOPE + MLA_V)
        k_nope, v_mla = kv[..., :MLA_NOPE], kv[..., MLA_NOPE:]
        k = jnp.concatenate([k_nope, jnp.broadcast_to(k_rope[:, :, None, :], (B, S, H_MLA, MLA_ROPE))], axis=-1)
        q = rms_norm(q, mla_qk_q_norm[l])
        k = rms_norm(k, mla_qk_k_norm[l])
        q = jnp.concatenate([q[..., :MLA_NOPE], apply_rope(q[..., MLA_NOPE:], cos, sin)], axis=-1)
        k = jnp.concatenate([k[..., :MLA_NOPE], apply_rope(k[..., MLA_NOPE:], cos, sin)], axis=-1)
        o_mla = rms_norm(dense_softmax_attention(q, k, v_mla), mla_out_norm[l])

        nq = rms_norm(na_q.reshape(B, S, H_NA, HEAD_DIM), na_qk_q_norm[l])
        nk = rms_norm(na_k.reshape(B, S, H_NA, HEAD_DIM), na_qk_k_norm[l])
        nv = na_v.reshape(B, S, H_NA, HEAD_DIM)
        o_na = rms_norm(neighbourhood_attention(nq, nk, nv, na_rpb[l]), na_out_norm[l])

        dq = rms_norm(df_q.reshape(B, S, H_DIFF, 2, DIFF_D), diff_qk_q_norm[l])
        dk = rms_norm(df_k.reshape(B, S, H_DIFF, 2, DIFF_D), diff_qk_k_norm[l])
        dv = df_v.reshape(B, S, H_DIFF, DIFF_V)
        lam_init = 0.8 - 0.6 * math.exp(-0.3 * l)
        lam = (jnp.exp(jnp.sum(diff_lambda_q1[l].astype(jnp.float32) * diff_lambda_k1[l].astype(jnp.float32)))
               - jnp.exp(jnp.sum(diff_lambda_q2[l].astype(jnp.float32) * diff_lambda_k2[l].astype(jnp.float32)))
               + lam_init)
        o_d = diff_attention(dq[..., 0, :], dq[..., 1, :], dk[..., 0, :], dk[..., 1, :], dv, lam, t5_table)
        o_diff = rms_norm(o_d.reshape(B, S, H_DIFF, DIFF_V), diff_subln[l]).reshape(B, S, W_DIFF) * (1.0 - lam_init)

        h = h + jnp.concatenate([o_mla, o_na, o_diff], axis=-1) @ w_out[l]

        h = h + 0.5 * swiglu(rms_norm(h, ffn2_norm[l]), ffn2_w_gate[l], ffn2_w_up[l], ffn2_w_down[l])
    return h
```

```python
import functools
import math

import numpy as np
import jax
import jax.numpy as jnp
from jax import lax
from jax.experimental import pallas as pl
from jax.experimental.pallas import tpu as pltpu

D_MODEL = 1024
HEAD_DIM = 64
H_MLA, H_NA, H_DIFF = 6, 5, 5
Q_LORA, KV_LORA = 256, 128
MLA_NOPE, MLA_ROPE = 64, 32
MLA_QK = MLA_NOPE + MLA_ROPE
ROPE_BASE = 10000.0
GRID_W = 64
NA_KR_MAX, NA_KC = 8, 16
DIFF_D = 32
T5_BUCKETS, T5_MAX_DIST = 32, 128
D_FF = 2816
EPS = 1e-6

LANES = 128
N_PAIR = 3
W_SLOT = N_PAIR * LANES
NP_IN = 512 + 6 * W_SLOT
NA_QROWS = 4
NA_WROWS = 12
MASK_VALUE = -1e30
VMEM_LIMIT = 56 * 1024 * 1024

BF16 = jnp.bfloat16
F32 = jnp.float32


def _dot(a, b):
    return jnp.dot(a, b, preferred_element_type=F32)


def _dot_nt(a, b):
    return lax.dot_general(a, b, (((1,), (1,)), ((), ())), preferred_element_type=F32)


def _const_spec(shape):
    nd = len(shape)
    return pl.BlockSpec(shape, lambda *_: (0,) * nd, pipeline_mode=pl.Buffered(1))


def _swiglu_half(h, g_ref, wg_ref, wu_ref, wd_ref):
    ms = jnp.mean(h * h, axis=-1, keepdims=True)
    xn = (h * lax.rsqrt(ms + EPS) * g_ref[...]).astype(BF16)
    gate = _dot(xn, wg_ref[...])
    up = _dot(xn, wu_ref[...])
    act = (gate * (1.0 / (1.0 + jnp.exp(-gate))) * up).astype(BF16)
    return h + 0.5 * _dot(act, wd_ref[...])


def _ffn_kernel(h_ref, g_ref, wg_ref, wu_ref, wd_ref, o_ref):
    o_ref[...] = _swiglu_half(h_ref[...], g_ref, wg_ref, wu_ref, wd_ref)


def _seg_mean_sq(x, ones_ref, seg):
    return _dot((x * x).astype(BF16), ones_ref[...]) * (1.0 / seg)


def _outproj_ffn_kernel(h_ref, om_ref, on_ref, od_ref, gm_ref, gn_ref, gd_ref, ones64_ref, wo_ref,
                        g_ref, wg_ref, wu_ref, wd_ref, o_ref, *, diff_out_scale):
    om = om_ref[...].astype(F32)
    on = on_ref[...].astype(F32)
    od = od_ref[...].astype(F32)
    om = om * lax.rsqrt(jnp.sum(om * om, axis=-1, keepdims=True) * (1.0 / (H_MLA * HEAD_DIM)) + EPS) * gm_ref[...]
    on = on * lax.rsqrt(jnp.sum(on * on, axis=-1, keepdims=True) * (1.0 / (H_NA * HEAD_DIM)) + EPS) * gn_ref[...]
    od = od * lax.rsqrt(_seg_mean_sq(od, ones64_ref, HEAD_DIM) + EPS) * (gd_ref[...] * diff_out_scale)
    cat = jnp.concatenate([om, on, od], axis=-1).astype(BF16)
    h = h_ref[...] + _dot(cat, wo_ref[...])
    o_ref[...] = _swiglu_half(h, g_ref, wg_ref, wu_ref, wd_ref)


def _token_tile(n_tokens, seq):
    tm = min(512, seq)
    assert seq % tm == 0 and n_tokens % tm == 0
    return tm


def _ffn(h, g, wg, wu, wd, seq):
    t, d = h.shape
    tm = _token_tile(t, seq)
    row = pl.BlockSpec((tm, d), lambda i: (i, 0))
    return pl.pallas_call(
        _ffn_kernel,
        grid=(t // tm,),
        in_specs=[row, _const_spec(g.shape), _const_spec(wg.shape), _const_spec(wu.shape), _const_spec(wd.shape)],
        out_specs=row,
        out_shape=jax.ShapeDtypeStruct((t, d), F32),
        compiler_params=pltpu.CompilerParams(dimension_semantics=("parallel",), vmem_limit_bytes=VMEM_LIMIT),
        name="ffn",
    )(h, g, wg, wu, wd)


def _outproj_ffn(h, om, on, od, gm, gn, gd, ones64, wo, g, wg, wu, wd, seq, diff_out_scale):
    t, d = h.shape
    tm = _token_tile(t, seq) // 2
    row = pl.BlockSpec((tm, d), lambda i: (i, 0))
    slot = pl.BlockSpec((tm, W_SLOT), lambda i: (i, 0))
    consts = [gm, gn, gd, ones64, wo, g, wg, wu, wd]
    return pl.pallas_call(
        functools.partial(_outproj_ffn_kernel, diff_out_scale=diff_out_scale),
        grid=(t // tm,),
        in_specs=[row, slot, slot, slot] + [_const_spec(c.shape) for c in consts],
        out_specs=row,
        out_shape=jax.ShapeDtypeStruct((t, d), F32),
        compiler_params=pltpu.CompilerParams(dimension_semantics=("parallel",), vmem_limit_bytes=VMEM_LIMIT),
        name="outproj_ffn",
    )(h, om, on, od, *consts)


def _prep_kernel(h_ref, gmix_ref, win_ref, gq_ref, wq_ref, gkv_ref, wk_ref,
                 gqc_ref, gqs_ref, gkc_ref, gks_ref, cos_ref, sin_ref,
                 ones64_ref, ones32_ref, gnq_ref, gnk_ref, gdq_ref, gdk_ref,
                 qm_ref, km_ref, vm_ref, nq_ref, nk_ref, nv_ref, dq_ref, dk_ref, dv_ref):
    h = h_ref[...]
    ms = jnp.mean(h * h, axis=-1, keepdims=True)
    xn = (h * lax.rsqrt(ms + EPS) * gmix_ref[...]).astype(BF16)
    u = _dot(xn, win_ref[...])

    cq = u[:, 0:Q_LORA]
    cqn = (cq * lax.rsqrt(jnp.mean(cq * cq, axis=-1, keepdims=True) + EPS) * gq_ref[...]).astype(BF16)
    q2 = _dot(cqn, wq_ref[...])
    ckv = u[:, Q_LORA:Q_LORA + KV_LORA]
    ckvn = ckv * lax.rsqrt(jnp.mean(ckv * ckv, axis=-1, keepdims=True) + EPS) * gkv_ref[...]
    kr = u[:, Q_LORA + KV_LORA:Q_LORA + 2 * KV_LORA]
    k2 = _dot(jnp.concatenate([ckvn, kr], axis=-1).astype(BF16), wk_ref[...])

    cos = cos_ref[...]
    sin = sin_ref[...]
    gqc = gqc_ref[...] * cos
    gqs = gqs_ref[...] * sin
    gkc = gkc_ref[...] * cos
    gks = gks_ref[...] * sin
    nsw = H_MLA * LANES
    for hd in range(H_MLA):
        lo, hi = hd * LANES, (hd + 1) * LANES
        qr, qs = q2[:, lo:hi], q2[:, nsw + lo:nsw + hi]
        inv = lax.rsqrt(jnp.sum(qr * qr, axis=-1, keepdims=True) * (1.0 / MLA_QK) + EPS) * (MLA_QK ** -0.5)
        qm_ref[:, lo:hi] = ((qr * gqc + qs * gqs) * inv).astype(BF16)
        kr_, ks_ = k2[:, lo:hi], k2[:, nsw + lo:nsw + hi]
        inv = lax.rsqrt(jnp.sum(kr_ * kr_, axis=-1, keepdims=True) * (1.0 / MLA_QK) + EPS)
        km_ref[:, lo:hi] = ((kr_ * gkc + ks_ * gks) * inv).astype(BF16)
    vm_ref[...] = k2[:, 2 * nsw:].astype(BF16)

    base = 2 * Q_LORA
    naq = u[:, base:base + W_SLOT]
    nak = u[:, base + W_SLOT:base + 2 * W_SLOT]
    nq_ref[...] = (naq * lax.rsqrt(_seg_mean_sq(naq, ones64_ref, HEAD_DIM) + EPS)
                   * (gnq_ref[...] * HEAD_DIM ** -0.5)).astype(BF16)
    nk_ref[...] = (nak * lax.rsqrt(_seg_mean_sq(nak, ones64_ref, HEAD_DIM) + EPS) * gnk_ref[...]).astype(BF16)
    nv_ref[...] = u[:, base + 2 * W_SLOT:base + 3 * W_SLOT].astype(BF16)
    base += 3 * W_SLOT
    dfq = u[:, base:base + W_SLOT]
    dfk = u[:, base + W_SLOT:base + 2 * W_SLOT]
    dq_ref[...] = (dfq * lax.rsqrt(_seg_mean_sq(dfq, ones32_ref, DIFF_D) + EPS)
                   * (gdq_ref[...] * DIFF_D ** -0.5)).astype(BF16)
    dk_ref[...] = (dfk * lax.rsqrt(_seg_mean_sq(dfk, ones32_ref, DIFF_D) + EPS) * gdk_ref[...]).astype(BF16)
    dv_ref[...] = u[:, base + 2 * W_SLOT:base + 3 * W_SLOT].astype(BF16)


def _prep(h, consts_a, cos, sin, consts_b, seq):
    t, d = h.shape
    tm = _token_tile(t, seq)
    tiles_per_seq = seq // tm
    row = pl.BlockSpec((tm, d), lambda i: (i, 0))
    pos = pl.BlockSpec((tm, LANES), lambda i: (i % tiles_per_seq, 0))
    widths = [H_MLA * LANES, H_MLA * LANES] + [W_SLOT] * 7
    return pl.pallas_call(
        _prep_kernel,
        grid=(t // tm,),
        in_specs=[row] + [_const_spec(c.shape) for c in consts_a] + [pos, pos]
        + [_const_spec(c.shape) for c in consts_b],
        out_specs=[pl.BlockSpec((tm, w), lambda i: (i, 0)) for w in widths],
        out_shape=[jax.ShapeDtypeStruct((t, w), BF16) for w in widths],
        compiler_params=pltpu.CompilerParams(dimension_semantics=("parallel",), vmem_limit_bytes=VMEM_LIMIT),
        name="mix_prep",
    )(h, *consts_a, cos, sin, *consts_b)


def _softmax_rows(s):
    m = jnp.max(s, axis=-1, keepdims=True)
    p = jnp.exp(s - m)
    return p, 1.0 / jnp.sum(p, axis=-1, keepdims=True)


def _low_half(shape):
    return lax.broadcasted_iota(jnp.int32, shape, len(shape) - 1) < HEAD_DIM


def _mla_attn_kernel(q_ref, k_ref, v_ref, o_ref):
    v = v_ref[0]
    outs = []
    for j in range(2):
        q = q_ref[0, :, j * LANES:(j + 1) * LANES]
        k = k_ref[0, :, j * LANES:(j + 1) * LANES]
        p, inv_l = _softmax_rows(_dot_nt(q, k))
        outs.append(_dot(p.astype(BF16), v) * inv_l)
    o_ref[0] = jnp.where(_low_half(outs[0].shape), outs[0], outs[1]).astype(o_ref.dtype)


def _mla_attn(q, k, v, tq):
    b, s, _ = q.shape
    return pl.pallas_call(
        _mla_attn_kernel,
        grid=(b, N_PAIR, s // tq),
        in_specs=[pl.BlockSpec((1, tq, 2 * LANES), lambda bi, p, qi: (bi, qi, p)),
                  pl.BlockSpec((1, s, 2 * LANES), lambda bi, p, qi: (bi, 0, p)),
                  pl.BlockSpec((1, s, LANES), lambda bi, p, qi: (bi, 0, p))],
        out_specs=pl.BlockSpec((1, tq, LANES), lambda bi, p, qi: (bi, qi, p)),
        out_shape=jax.ShapeDtypeStruct((b, s, W_SLOT), BF16),
        compiler_params=pltpu.CompilerParams(dimension_semantics=("parallel", "parallel", "parallel"),
                                             vmem_limit_bytes=VMEM_LIMIT),
        name="mla_attn",
    )(q, k, v)


def _diff_attn_kernel(lam_ref, q_ref, k_ref, v_ref, bias_ref, o_ref):
    lam = lam_ref[0]
    q = q_ref[0]
    k = k_ref[0]
    v = v_ref[0]
    lane = lax.broadcasted_iota(jnp.int32, q.shape, 1)
    zero = jnp.zeros_like(q)
    outs = []
    for j in range(2):
        bias = bias_ref[j]
        maps = []
        for m in range(2):
            lo = (2 * j + m) * DIFF_D
            qm = jnp.where((lane >= lo) & (lane < lo + DIFF_D), q, zero)
            maps.append(_softmax_rows(_dot_nt(qm, k) + bias))
        (p1, il1), (p2, il2) = maps
        a = p1 * il1 - p2 * (il2 * lam)
        outs.append(_dot(a.astype(BF16), v))
    o_ref[0] = jnp.where(_low_half(outs[0].shape), outs[0], outs[1]).astype(o_ref.dtype)


def _diff_attn(lam, q, k, v, bias, tq):
    b, s, _ = q.shape
    return pl.pallas_call(
        _diff_attn_kernel,
        grid=(N_PAIR, s // tq, b),
        in_specs=[pl.BlockSpec(memory_space=pltpu.SMEM),
                  pl.BlockSpec((1, tq, LANES), lambda p, qi, bi: (bi, qi, p)),
                  pl.BlockSpec((1, s, LANES), lambda p, qi, bi: (bi, 0, p)),
                  pl.BlockSpec((1, s, LANES), lambda p, qi, bi: (bi, 0, p)),
                  pl.BlockSpec((2, tq, s), lambda p, qi, bi: (p, qi, 0))],
        out_specs=pl.BlockSpec((1, tq, LANES), lambda p, qi, bi: (bi, qi, p)),
        out_shape=jax.ShapeDtypeStruct((b, s, W_SLOT), BF16),
        compiler_params=pltpu.CompilerParams(dimension_semantics=("parallel", "parallel", "parallel"),
                                             vmem_limit_bytes=VMEM_LIMIT),
        name="diff_attn",
    )(lam, q, k, v, bias)


def _na_plan(seq):
    rows = seq // GRID_W
    kr = min(NA_KR_MAX, rows)
    assert rows % NA_QROWS == 0 and rows >= NA_WROWS
    patterns, blocks = [], []
    for r0 in range(0, rows, NA_QROWS):
        ws = int(np.clip(r0 - kr // 2, 0, rows - NA_WROWS))
        rs = [int(np.clip(r0 + j - kr // 2, 0, rows - kr)) for j in range(NA_QROWS)]
        assert all(ws <= x and x + kr <= ws + NA_WROWS for x in rs)
        pat = (r0 - ws, tuple(x - ws for x in rs))
        if pat not in patterns:
            patterns.append(pat)
        blocks.append((r0, ws, patterns.index(pat)))
    return kr, patterns, blocks


def _na_bias_table(rpb, seq):
    kr, patterns, _ = _na_plan(seq)
    nq, nk = NA_QROWS * GRID_W, NA_WROWS * GRID_W
    qj, qc = np.divmod(np.arange(nq), GRID_W)
    ki, kc = np.divmod(np.arange(nk), GRID_W)
    cstart = np.clip(qc - NA_KC // 2, 0, GRID_W - NA_KC)
    col_ok = (kc[None, :] >= cstart[:, None]) & (kc[None, :] < cstart[:, None] + NA_KC)
    col_off = np.clip(kc[None, :] - qc[:, None] + NA_KC - 1, 0, 2 * NA_KC - 2)
    tabs = []
    for d0, rs_rel in patterns:
        rs_q = np.asarray(rs_rel)[qj]
        row_ok = (ki[None, :] >= rs_q[:, None]) & (ki[None, :] < rs_q[:, None] + kr)
        row_off = np.clip(ki[None, :] - (d0 + qj[:, None]) + NA_KR_MAX - 1, 0, 2 * NA_KR_MAX - 2)
        vals = rpb[:, row_off, col_off]
        tabs.append(jnp.where(jnp.asarray(row_ok & col_ok)[None], vals, MASK_VALUE))
    tab = jnp.stack(tabs, axis=1)
    tab = jnp.concatenate([tab, jnp.zeros((2 * N_PAIR - H_NA,) + tab.shape[1:], F32)], axis=0)
    return tab.reshape(N_PAIR, 2, len(patterns), nq, nk)


def _na_attn_kernel(q_ref, k_ref, v_ref, bias_ref, o_ref, *, blocks):
    nq, nk = NA_QROWS * GRID_W, NA_WROWS * GRID_W
    lane = lax.broadcasted_iota(jnp.int32, (nq, LANES), 1)
    for r0, ws, pat in blocks:
        q = q_ref[0, r0 * GRID_W:r0 * GRID_W + nq, :]
        k = k_ref[0, ws * GRID_W:ws * GRID_W + nk, :]
        v = v_ref[0, ws * GRID_W:ws * GRID_W + nk, :]
        outs = []
        for j in range(2):
            qj = jnp.where((lane >= j * HEAD_DIM) & (lane < (j + 1) * HEAD_DIM), q, jnp.zeros_like(q))
            p, inv_l = _softmax_rows(_dot_nt(qj, k) + bias_ref[0, j, pat])
            outs.append(_dot(p.astype(BF16), v) * inv_l)
        o_ref[0, r0 * GRID_W:r0 * GRID_W + nq, :] = jnp.where(lane < HEAD_DIM, outs[0], outs[1]).astype(o_ref.dtype)


def _na_attn(q, k, v, bias_tab):
    b, s, _ = q.shape
    _, _, blocks = _na_plan(s)
    tok = pl.BlockSpec((1, s, LANES), lambda p, bi: (bi, 0, p))
    return pl.pallas_call(
        functools.partial(_na_attn_kernel, blocks=tuple(blocks)),
        grid=(N_PAIR, b),
        in_specs=[tok, tok, tok,
                  pl.BlockSpec((1,) + bias_tab.shape[1:], lambda p, bi: (p, 0, 0, 0, 0))],
        out_specs=tok,
        out_shape=jax.ShapeDtypeStruct((b, s, W_SLOT), BF16),
        compiler_params=pltpu.CompilerParams(dimension_semantics=("parallel", "parallel"),
                                             vmem_limit_bytes=VMEM_LIMIT),
        name="na_attn",
    )(q, k, v, bias_tab)


def _pad_cols(w, groups):
    parts = []
    for start, width, padded in groups:
        parts.append(w[..., start:start + width])
        if padded > width:
            parts.append(jnp.zeros(w.shape[:-1] + (padded - width,), w.dtype))
    return jnp.concatenate(parts, axis=-1)


def _t5_bucket(rel):
    half = T5_BUCKETS // 2
    max_exact = half // 2
    n = jnp.abs(rel)
    nf = jnp.maximum(n, 1).astype(F32)
    large = max_exact + (jnp.log(nf / max_exact) / math.log(T5_MAX_DIST / max_exact)
                         * (half - max_exact)).astype(jnp.int32)
    large = jnp.minimum(large, half - 1)
    return jnp.where(rel > 0, half, 0) + jnp.where(n < max_exact, n, large)


def _t5_bias(t5_table, seq):
    m = jnp.arange(2 * seq, dtype=jnp.int32)
    rel = jnp.where(m < seq, m, m - 2 * seq)
    vec = t5_table[_t5_bucket(rel)].T
    h = vec.shape[0]
    skew = jnp.tile(vec, (1, seq))[:, :seq * (2 * seq - 1)].reshape(h, seq, 2 * seq - 1)[:, :, :seq]
    return jnp.concatenate([skew, jnp.zeros((2 * N_PAIR - h, seq, seq), F32)], axis=0)


def _swap_rope_lanes(w):
    r = MLA_ROPE // 2
    return jnp.concatenate([w[..., :MLA_NOPE], w[..., MLA_NOPE + r:], w[..., MLA_NOPE:MLA_NOPE + r]], axis=-1)


def _pad_lanes(w, width=LANES):
    return jnp.concatenate([w, jnp.zeros(w.shape[:-1] + (width - w.shape[-1],), w.dtype)], axis=-1)


def _block_ones(width, seg):
    idx = np.arange(width) // seg
    return jnp.asarray(idx[:, None] == idx[None, :], dtype=BF16)


def kernel(x, ffn1_norm, ffn1_w_gate, ffn1_w_up, ffn1_w_down, mix_norm, w_in, mla_q_norm, mla_w_uq, mla_kv_norm, mla_w_ukv, mla_qk_q_norm, mla_qk_k_norm, mla_out_norm, na_qk_q_norm, na_qk_k_norm, na_rpb, na_out_norm, diff_qk_q_norm, diff_qk_k_norm, diff_lambda_q1, diff_lambda_k1, diff_lambda_q2, diff_lambda_k2, diff_subln, t5_table, w_out, ffn2_norm, ffn2_w_gate, ffn2_w_up, ffn2_w_down):
    b, s, d = x.shape
    depth = w_in.shape[0]
    t = b * s
    tq = min(256, s)

    inv_freq = ROPE_BASE ** (-jnp.arange(0, MLA_ROPE, 2, dtype=F32) / MLA_ROPE)
    ang = jnp.arange(s, dtype=F32)[:, None] * inv_freq[None, :]
    cos, sin = jnp.cos(ang), jnp.sin(ang)
    cos_tab = jnp.concatenate([jnp.ones((s, MLA_NOPE), F32), cos, cos, jnp.zeros((s, LANES - MLA_QK), F32)], axis=-1)
    sin_tab = jnp.concatenate([jnp.zeros((s, MLA_NOPE), F32), -sin, sin, jnp.zeros((s, LANES - MLA_QK), F32)], axis=-1)

    t5_bias = _t5_bias(t5_table, s)
    ones64 = _block_ones(W_SLOT, HEAD_DIM)
    ones32 = _block_ones(W_SLOT, DIFF_D)
    seg = H_NA * HEAD_DIM
    in_groups = [(0, Q_LORA + KV_LORA + MLA_ROPE, 2 * Q_LORA)] + [
        (Q_LORA + KV_LORA + MLA_ROPE + i * seg, seg, W_SLOT) for i in range(6)]

    h = x.reshape(t, d)
    for l in range(depth):
        row = lambda v: v[l][None, :]
        h = _ffn(h, row(ffn1_norm), ffn1_w_gate[l].astype(BF16), ffn1_w_up[l].astype(BF16),
                 ffn1_w_down[l].astype(BF16), s)

        win = _pad_cols(w_in[l], in_groups).astype(BF16)
        wuq = mla_w_uq[l].reshape(Q_LORA, H_MLA, MLA_QK)
        wq_all = jnp.concatenate([_pad_lanes(wuq).reshape(Q_LORA, -1),
                                  _pad_lanes(_swap_rope_lanes(wuq)).reshape(Q_LORA, -1)], axis=-1).astype(BF16)
        wukv = mla_w_ukv[l].reshape(KV_LORA, H_MLA, MLA_NOPE + HEAD_DIM)
        eye = jnp.broadcast_to(jnp.eye(MLA_ROPE, dtype=F32)[:, None, :], (MLA_ROPE, H_MLA, MLA_ROPE))
        k_top = jnp.concatenate([wukv[..., :MLA_NOPE], jnp.zeros((KV_LORA, H_MLA, MLA_ROPE), F32)], axis=-1)
        k_bot = jnp.concatenate([jnp.zeros((MLA_ROPE, H_MLA, MLA_NOPE), F32), eye], axis=-1)
        k_rows = jnp.concatenate([k_top, k_bot, jnp.zeros((KV_LORA - MLA_ROPE, H_MLA, MLA_QK), F32)], axis=0)
        v_rows = jnp.concatenate([wukv[..., MLA_NOPE:].reshape(KV_LORA, -1),
                                  jnp.zeros((KV_LORA, H_MLA * HEAD_DIM), F32)], axis=0)
        wk_all = jnp.concatenate([_pad_lanes(k_rows).reshape(2 * KV_LORA, -1),
                                  _pad_lanes(_swap_rope_lanes(k_rows)).reshape(2 * KV_LORA, -1),
                                  v_rows], axis=-1).astype(BF16)
        gq, gk = mla_qk_q_norm[l], mla_qk_k_norm[l]
        tile_pairs = lambda g, n: _pad_lanes(jnp.tile(g, n), W_SLOT)[None, :]
        consts_a = [row(mix_norm), win, row(mla_q_norm), wq_all, row(mla_kv_norm), wk_all,
                    _pad_lanes(gq)[None, :], _pad_lanes(_swap_rope_lanes(gq))[None, :],
                    _pad_lanes(gk)[None, :], _pad_lanes(_swap_rope_lanes(gk))[None, :]]
        consts_b = [ones64, ones32, tile_pairs(na_qk_q_norm[l], H_NA), tile_pairs(na_qk_k_norm[l], H_NA),
                    tile_pairs(diff_qk_q_norm[l], 2 * H_DIFF), tile_pairs(diff_qk_k_norm[l], 2 * H_DIFF)]
        qm, km, vm, nq, nk, nv, dq, dk, dv = _prep(h, consts_a, cos_tab, sin_tab, consts_b, s)
        seq3 = lambda a: a.reshape(b, s, a.shape[-1])

        o_mla = _mla_attn(seq3(qm), seq3(km), seq3(vm), tq)
        o_na = _na_attn(seq3(nq), seq3(nk), seq3(nv), _na_bias_table(na_rpb[l], s))
        lam_init = 0.8 - 0.6 * math.exp(-0.3 * l)
        lam = (jnp.exp(jnp.sum(diff_lambda_q1[l] * diff_lambda_k1[l]))
               - jnp.exp(jnp.sum(diff_lambda_q2[l] * diff_lambda_k2[l])) + lam_init).reshape(1)
        o_diff = _diff_attn(lam, seq3(dq), seq3(dk), seq3(dv), t5_bias, tq)

        wo = w_out[l]
        wo_all = jnp.concatenate([wo[:H_MLA * HEAD_DIM],
                                  wo[H_MLA * HEAD_DIM:H_MLA * HEAD_DIM + seg], jnp.zeros((W_SLOT - seg, d), F32),
                                  wo[H_MLA * HEAD_DIM + seg:], jnp.zeros((W_SLOT - seg, d), F32)], axis=0).astype(BF16)
        h = _outproj_ffn(h, o_mla.reshape(t, -1), o_na.reshape(t, -1), o_diff.reshape(t, -1),
                         row(mla_out_norm), _pad_lanes(na_out_norm[l], W_SLOT)[None, :],
                         tile_pairs(diff_subln[l], H_DIFF), ones64, wo_all,
                         row(ffn2_norm), ffn2_w_gate[l].astype(BF16), ffn2_w_up[l].astype(BF16),
                         ffn2_w_down[l].astype(BF16), s, 1.0 - lam_init)
    return h.reshape(b, s, d)
```

```python
import functools
import math

import numpy as np
import jax
import jax.numpy as jnp
from jax import lax
from jax.experimental import pallas as pl
from jax.experimental.pallas import tpu as pltpu

D_MODEL = 1024
HEAD_DIM = 64
H_MLA, H_NA, H_DIFF = 6, 5, 5
Q_LORA, KV_LORA = 256, 128
MLA_NOPE, MLA_ROPE = 64, 32
MLA_QK = MLA_NOPE + MLA_ROPE
ROPE_BASE = 10000.0
GRID_W = 64
NA_KR_MAX, NA_KC = 8, 16
DIFF_D = 32
T5_BUCKETS, T5_MAX_DIST = 32, 128
D_FF = 2816
EPS = 1e-6

LANES = 128
N_PAIR = 3
W_SLOT = N_PAIR * LANES
NP_IN = 512 + 6 * W_SLOT
NA_QROWS = 4
NA_WROWS = 12
MASK_VALUE = -1e30
VMEM_LIMIT = 56 * 1024 * 1024

BF16 = jnp.bfloat16
F32 = jnp.float32


def _dot(a, b):
    return jnp.dot(a, b, preferred_element_type=F32)


def _dot_nt(a, b):
    return lax.dot_general(a, b, (((1,), (1,)), ((), ())), preferred_element_type=F32)


def _const_spec(shape):
    nd = len(shape)
    return pl.BlockSpec(shape, lambda *_: (0,) * nd, pipeline_mode=pl.Buffered(1))


def _swiglu_half(h, g_ref, wg_ref, wu_ref, wd_ref):
    ms = jnp.mean(h * h, axis=-1, keepdims=True)
    xn = (h * lax.rsqrt(ms + EPS) * g_ref[...]).astype(BF16)
    gate = _dot(xn, wg_ref[...])
    up = _dot(xn, wu_ref[...])
    act = (gate * (1.0 / (1.0 + jnp.exp(-gate))) * up).astype(BF16)
    return h + 0.5 * _dot(act, wd_ref[...])


def _ffn_kernel(h_ref, g_ref, wg_ref, wu_ref, wd_ref, o_ref):
    o_ref[...] = _swiglu_half(h_ref[...], g_ref, wg_ref, wu_ref, wd_ref)


def _seg_mean_sq(x, ones_ref, seg):
    return _dot((x * x).astype(BF16), ones_ref[...]) * (1.0 / seg)


def _outproj_ffn_kernel(h_ref, om_ref, on_ref, od_ref, gm_ref, gn_ref, gd_ref, ones64_ref, wo_ref,
                        g_ref, wg_ref, wu_ref, wd_ref, o_ref, *, diff_out_scale):
    om = om_ref[...].astype(F32)
    on = on_ref[...].astype(F32)
    od = od_ref[...].astype(F32)
    om = om * lax.rsqrt(jnp.sum(om * om, axis=-1, keepdims=True) * (1.0 / (H_MLA * HEAD_DIM)) + EPS) * gm_ref[...]
    on = on * lax.rsqrt(jnp.sum(on * on, axis=-1, keepdims=True) * (1.0 / (H_NA * HEAD_DIM)) + EPS) * gn_ref[...]
    od = od * lax.rsqrt(_seg_mean_sq(od, ones64_ref, HEAD_DIM) + EPS) * (gd_ref[...] * diff_out_scale)
    cat = jnp.concatenate([om, on, od], axis=-1).astype(BF16)
    h = h_ref[...] + _dot(cat, wo_ref[...])
    o_ref[...] = _swiglu_half(h, g_ref, wg_ref, wu_ref, wd_ref)


def _token_tile(n_tokens, seq):
    tm = min(512, seq)
    assert seq % tm == 0 and n_tokens % tm == 0
    return tm


def _ffn(h, g, wg, wu, wd, seq):
    t, d = h.shape
    tm = _token_tile(t, seq)
    row = pl.BlockSpec((tm, d), lambda i: (i, 0))
    return pl.pallas_call(
        _ffn_kernel,
        grid=(t // tm,),
        in_specs=[row, _const_spec(g.shape), _const_spec(wg.shape), _const_spec(wu.shape), _const_spec(wd.shape)],
        out_specs=row,
        out_shape=jax.ShapeDtypeStruct((t, d), F32),
        compiler_params=pltpu.CompilerParams(dimension_semantics=("parallel",), vmem_limit_bytes=VMEM_LIMIT),
        name="ffn",
    )(h, g, wg, wu, wd)


def _outproj_ffn(h, om, on, od, gm, gn, gd, ones64, wo, g, wg, wu, wd, seq, diff_out_scale):
    t, d = h.shape
    tm = _token_tile(t, seq) // 2
    row = pl.BlockSpec((tm, d), lambda i: (i, 0))
    slot = pl.BlockSpec((tm, W_SLOT), lambda i: (i, 0))
    consts = [gm, gn, gd, ones64, wo, g, wg, wu, wd]
    return pl.pallas_call(
        functools.partial(_outproj_ffn_kernel, diff_out_scale=diff_out_scale),
        grid=(t // tm,),
        in_specs=[row, slot, slot, slot] + [_const_spec(c.shape) for c in consts],
        out_specs=row,
        out_shape=jax.ShapeDtypeStruct((t, d), F32),
        compiler_params=pltpu.CompilerParams(dimension_semantics=("parallel",), vmem_limit_bytes=VMEM_LIMIT),
        name="outproj_ffn",
    )(h, om, on, od, *consts)


def _prep_kernel(h_ref, gmix_ref, win_ref, gq_ref, wq_ref, gkv_ref, wk_ref,
                 gqc_ref, gqs_ref, gkc_ref, gks_ref, cos_ref, sin_ref,
                 ones64_ref, ones32_ref, gnq_ref, gnk_ref, gdq_ref, gdk_ref,
                 qm_ref, km_ref, vm_ref, nq_ref, nk_ref, nv_ref, dq_ref, dk_ref, dv_ref):
    h = h_ref[...]
    ms = jnp.mean(h * h, axis=-1, keepdims=True)
    xn = (h * lax.rsqrt(ms + EPS) * gmix_ref[...]).astype(BF16)
    u = _dot(xn, win_ref[...])

    cq = u[:, 0:Q_LORA]
    cqn = (cq * lax.rsqrt(jnp.mean(cq * cq, axis=-1, keepdims=True) + EPS) * gq_ref[...]).astype(BF16)
    q2 = _dot(cqn, wq_ref[...])
    ckv = u[:, Q_LORA:Q_LORA + KV_LORA]
    ckvn = ckv * lax.rsqrt(jnp.mean(ckv * ckv, axis=-1, keepdims=True) + EPS) * gkv_ref[...]
    kr = u[:, Q_LORA + KV_LORA:Q_LORA + 2 * KV_LORA]
    k2 = _dot(jnp.concatenate([ckvn, kr], axis=-1).astype(BF16), wk_ref[...])

    cos = cos_ref[...]
    sin = sin_ref[...]
    gqc = gqc_ref[...] * cos
    gqs = gqs_ref[...] * sin
    gkc = gkc_ref[...] * cos
    gks = gks_ref[...] * sin
    nsw = H_MLA * LANES
    for hd in range(H_MLA):
        lo, hi = hd * LANES, (hd + 1) * LANES
        qr, qs = q2[:, lo:hi], q2[:, nsw + lo:nsw + hi]
        inv = lax.rsqrt(jnp.sum(qr * qr, axis=-1, keepdims=True) * (1.0 / MLA_QK) + EPS) * (MLA_QK ** -0.5)
        qm_ref[:, lo:hi] = ((qr * gqc + qs * gqs) * inv).astype(BF16)
        kr_, ks_ = k2[:, lo:hi], k2[:, nsw + lo:nsw + hi]
        inv = lax.rsqrt(jnp.sum(kr_ * kr_, axis=-1, keepdims=True) * (1.0 / MLA_QK) + EPS)
        km_ref[:, lo:hi] = ((kr_ * gkc + ks_ * gks) * inv).astype(BF16)
    vm_ref[...] = k2[:, 2 * nsw:].astype(BF16)

    base = 2 * Q_LORA
    naq = u[:, base:base + W_SLOT]
    nak = u[:, base + W_SLOT:base + 2 * W_SLOT]
    nq_ref[...] = (naq * lax.rsqrt(_seg_mean_sq(naq, ones64_ref, HEAD_DIM) + EPS)
                   * (gnq_ref[...] * HEAD_DIM ** -0.5)).astype(BF16)
    nk_ref[...] = (nak * lax.rsqrt(_seg_mean_sq(nak, ones64_ref, HEAD_DIM) + EPS) * gnk_ref[...]).astype(BF16)
    nv_ref[...] = u[:, base + 2 * W_SLOT:base + 3 * W_SLOT].astype(BF16)
    base += 3 * W_SLOT
    dfq = u[:, base:base + W_SLOT]
    dfk = u[:, base + W_SLOT:base + 2 * W_SLOT]
    dq_ref[...] = (dfq * lax.rsqrt(_seg_mean_sq(dfq, ones32_ref, DIFF_D) + EPS)
                   * (gdq_ref[...] * DIFF_D ** -0.5)).astype(BF16)
    dk_ref[...] = (dfk * lax.rsqrt(_seg_mean_sq(dfk, ones32_ref, DIFF_D) + EPS) * gdk_ref[...]).astype(BF16)
    dv_ref[...] = u[:, base + 2 * W_SLOT:base + 3 * W_SLOT].astype(BF16)


def _prep(h, consts_a, cos, sin, consts_b, seq):
    t, d = h.shape
    tm = _token_tile(t, seq)
    tiles_per_seq = seq // tm
    row = pl.BlockSpec((tm, d), lambda i: (i, 0))
    pos = pl.BlockSpec((tm, LANES), lambda i: (i % tiles_per_seq, 0))
    widths = [H_MLA * LANES, H_MLA * LANES] + [W_SLOT] * 7
    return pl.pallas_call(
        _prep_kernel,
        grid=(t // tm,),
        in_specs=[row] + [_const_spec(c.shape) for c in consts_a] + [pos, pos]
        + [_const_spec(c.shape) for c in consts_b],
        out_specs=[pl.BlockSpec((tm, w), lambda i: (i, 0)) for w in widths],
        out_shape=[jax.ShapeDtypeStruct((t, w), BF16) for w in widths],
        compiler_params=pltpu.CompilerParams(dimension_semantics=("parallel",), vmem_limit_bytes=VMEM_LIMIT),
        name="mix_prep",
    )(h, *consts_a, cos, sin, *consts_b)


def _softmax_rows(s):
    m = jnp.max(s, axis=-1, keepdims=True)
    p = jnp.exp(s - m)
    return p, 1.0 / jnp.sum(p, axis=-1, keepdims=True)


def _low_half(shape):
    return lax.broadcasted_iota(jnp.int32, shape, len(shape) - 1) < HEAD_DIM


def _mla_attn_kernel(q_ref, k_ref, v_ref, o_ref):
    v = v_ref[0]
    outs = []
    for j in range(2):
        q = q_ref[0, :, j * LANES:(j + 1) * LANES]
        k = k_ref[0, :, j * LANES:(j + 1) * LANES]
        p, inv_l = _softmax_rows(_dot_nt(q, k))
        outs.append(_dot(p.astype(BF16), v) * inv_l)
    o_ref[0] = jnp.where(_low_half(outs[0].shape), outs[0], outs[1]).astype(o_ref.dtype)


def _mla_attn(q, k, v, tq):
    b, s, _ = q.shape
    return pl.pallas_call(
        _mla_attn_kernel,
        grid=(b, N_PAIR, s // tq),
        in_specs=[pl.BlockSpec((1, tq, 2 * LANES), lambda bi, p, qi: (bi, qi, p)),
                  pl.BlockSpec((1, s, 2 * LANES), lambda bi, p, qi: (bi, 0, p)),
                  pl.BlockSpec((1, s, LANES), lambda bi, p, qi: (bi, 0, p))],
        out_specs=pl.BlockSpec((1, tq, LANES), lambda bi, p, qi: (bi, qi, p)),
        out_shape=jax.ShapeDtypeStruct((b, s, W_SLOT), BF16),
        compiler_params=pltpu.CompilerParams(dimension_semantics=("parallel", "parallel", "parallel"),
                                             vmem_limit_bytes=VMEM_LIMIT),
        name="mla_attn",
    )(q, k, v)


def _diff_attn_kernel(lam_ref, q_ref, k_ref, v_ref, bias_ref, o_ref):
    lam = lam_ref[0]
    q = q_ref[0]
    k = k_ref[0]
    v = v_ref[0]
    lane = lax.broadcasted_iota(jnp.int32, q.shape, 1)
    zero = jnp.zeros_like(q)
    outs = []
    for j in range(2):
        bias = bias_ref[j]
        maps = []
        for m in range(2):
            lo = (2 * j + m) * DIFF_D
            qm = jnp.where((lane >= lo) & (lane < lo + DIFF_D), q, zero)
            maps.append(_softmax_rows(_dot_nt(qm, k) + bias))
        (p1, il1), (p2, il2) = maps
        a = p1 * il1 - p2 * (il2 * lam)
        outs.append(_dot(a.astype(BF16), v))
    o_ref[0] = jnp.where(_low_half(outs[0].shape), outs[0], outs[1]).astype(o_ref.dtype)


def _diff_attn(lam, q, k, v, bias, tq):
    b, s, _ = q.shape
    return pl.pallas_call(
        _diff_attn_kernel,
        grid=(N_PAIR, s // tq, b),
        in_specs=[pl.BlockSpec(memory_space=pltpu.SMEM),
                  pl.BlockSpec((1, tq, LANES), lambda p, qi, bi: (bi, qi, p)),
                  pl.BlockSpec((1, s, LANES), lambda p, qi, bi: (bi, 0, p)),
                  pl.BlockSpec((1, s, LANES), lambda p, qi, bi: (bi, 0, p)),
                  pl.BlockSpec((2, tq, s), lambda p, qi, bi: (p, qi, 0))],
        out_specs=pl.BlockSpec((1, tq, LANES), lambda p, qi, bi: (bi, qi, p)),
        out_shape=jax.ShapeDtypeStruct((b, s, W_SLOT), BF16),
        compiler_params=pltpu.CompilerParams(dimension_semantics=("parallel", "parallel", "parallel"),
                                             vmem_limit_bytes=VMEM_LIMIT),
        name="diff_attn",
    )(lam, q, k, v, bias)


def _na_plan(seq):
    rows = seq // GRID_W
    kr = min(NA_KR_MAX, rows)
    assert rows % NA_QROWS == 0 and rows >= NA_WROWS
    patterns, blocks = [], []
    for r0 in range(0, rows, NA_QROWS):
        ws = int(np.clip(r0 - kr // 2, 0, rows - NA_WROWS))
        rs = [int(np.clip(r0 + j - kr // 2, 0, rows - kr)) for j in range(NA_QROWS)]
        assert all(ws <= x and x + kr <= ws + NA_WROWS for x in rs)
        pat = (r0 - ws, tuple(x - ws for x in rs))
        if pat not in patterns:
            patterns.append(pat)
        blocks.append((r0, ws, patterns.index(pat)))
    return kr, patterns, blocks


def _na_bias_table(rpb, seq):
    kr, patterns, _ = _na_plan(seq)
    nq, nk = NA_QROWS * GRID_W, NA_WROWS * GRID_W
    c = np.arange(GRID_W)
    cstart = np.clip(c - NA_KC // 2, 0, GRID_W - NA_KC)
    col_ok = (c[None, :] >= cstart[:, None]) & (c[None, :] < cstart[:, None] + NA_KC)
    col_sel = (c[None, :, None] - c[:, None, None] + NA_KC - 1) == np.arange(2 * NA_KC - 1)
    j, i = np.arange(NA_QROWS), np.arange(NA_WROWS)
    row_ok, row_sel = [], []
    for d0, rs_rel in patterns:
        rs = np.asarray(rs_rel)
        row_ok.append((i[None, :] >= rs[:, None]) & (i[None, :] < rs[:, None] + kr))
        row_sel.append((i[None, :, None] - (d0 + j[:, None, None]) + NA_KR_MAX - 1) == np.arange(2 * NA_KR_MAX - 1))
    row_ok, row_sel = np.stack(row_ok), np.stack(row_sel)
    rows = jnp.einsum('pjia,hab->hpjib', jnp.asarray(row_sel, F32), rpb, precision=lax.Precision.HIGHEST)
    vals = jnp.einsum('hpjib,ckb->hpjcik', rows, jnp.asarray(col_sel, F32), precision=lax.Precision.HIGHEST)
    ok = row_ok[:, :, None, :, None] & col_ok[None, None, :, None, :]
    tab = jnp.where(jnp.asarray(ok)[None], vals, MASK_VALUE).reshape(H_NA, len(patterns), nq, nk)
    tab = jnp.concatenate([tab, jnp.zeros((2 * N_PAIR - H_NA,) + tab.shape[1:], F32)], axis=0)
    return tab.reshape(N_PAIR, 2, len(patterns), nq, nk)


def _na_attn_kernel(q_ref, k_ref, v_ref, bias_ref, o_ref, *, blocks):
    nq, nk = NA_QROWS * GRID_W, NA_WROWS * GRID_W
    lane = lax.broadcasted_iota(jnp.int32, (nq, LANES), 1)
    for r0, ws, pat in blocks:
        q = q_ref[0, r0 * GRID_W:r0 * GRID_W + nq, :]
        k = k_ref[0, ws * GRID_W:ws * GRID_W + nk, :]
        v = v_ref[0, ws * GRID_W:ws * GRID_W + nk, :]
        outs = []
        for j in range(2):
            qj = jnp.where((lane >= j * HEAD_DIM) & (lane < (j + 1) * HEAD_DIM), q, jnp.zeros_like(q))
            p, inv_l = _softmax_rows(_dot_nt(qj, k) + bias_ref[0, j, pat])
            outs.append(_dot(p.astype(BF16), v) * inv_l)
        o_ref[0, r0 * GRID_W:r0 * GRID_W + nq, :] = jnp.where(lane < HEAD_DIM, outs[0], outs[1]).astype(o_ref.dtype)


def _na_attn(q, k, v, bias_tab):
    b, s, _ = q.shape
    _, _, blocks = _na_plan(s)
    tok = pl.BlockSpec((1, s, LANES), lambda p, bi: (bi, 0, p))
    return pl.pallas_call(
        functools.partial(_na_attn_kernel, blocks=tuple(blocks)),
        grid=(N_PAIR, b),
        in_specs=[tok, tok, tok,
                  pl.BlockSpec((1,) + bias_tab.shape[1:], lambda p, bi: (p, 0, 0, 0, 0))],
        out_specs=tok,
        out_shape=jax.ShapeDtypeStruct((b, s, W_SLOT), BF16),
        compiler_params=pltpu.CompilerParams(dimension_semantics=("parallel", "parallel"),
                                             vmem_limit_bytes=VMEM_LIMIT),
        name="na_attn",
    )(q, k, v, bias_tab)


def _pad_cols(w, groups):
    parts = []
    for start, width, padded in groups:
        parts.append(w[..., start:start + width])
        if padded > width:
            parts.append(jnp.zeros(w.shape[:-1] + (padded - width,), w.dtype))
    return jnp.concatenate(parts, axis=-1)


def _t5_bucket(rel):
    half = T5_BUCKETS // 2
    max_exact = half // 2
    n = jnp.abs(rel)
    nf = jnp.maximum(n, 1).astype(F32)
    large = max_exact + (jnp.log(nf / max_exact) / math.log(T5_MAX_DIST / max_exact)
                         * (half - max_exact)).astype(jnp.int32)
    large = jnp.minimum(large, half - 1)
    return jnp.where(rel > 0, half, 0) + jnp.where(n < max_exact, n, large)


def _t5_bias_kernel(thr_ref, tab_ref, o_ref):
    half = T5_BUCKETS // 2
    head = pl.program_id(0)
    tq, s = o_ref.shape[1:]
    q = pl.program_id(1) * tq + lax.broadcasted_iota(jnp.int32, (tq, s), 0)
    rel = lax.broadcasted_iota(jnp.int32, (tq, s), 1) - q
    n = jnp.abs(rel)
    neg = jnp.full((tq, s), tab_ref[head * T5_BUCKETS], F32)
    pos = jnp.full((tq, s), tab_ref[head * T5_BUCKETS + half], F32)
    for j in range(1, half):
        ge = n >= thr_ref[j]
        neg = jnp.where(ge, tab_ref[head * T5_BUCKETS + j], neg)
        pos = jnp.where(ge, tab_ref[head * T5_BUCKETS + half + j], pos)
    o_ref[0] = jnp.where(rel > 0, pos, neg)


def _t5_bias(t5_table, seq, tq):
    half = T5_BUCKETS // 2
    dist_bucket = _t5_bucket(-jnp.arange(seq, dtype=jnp.int32))
    thr = jnp.sum(dist_bucket[None, :] < jnp.arange(half, dtype=jnp.int32)[:, None], axis=1).astype(jnp.int32)
    tab = jnp.concatenate([t5_table.T, jnp.zeros((2 * N_PAIR - t5_table.shape[1], T5_BUCKETS), F32)], axis=0)
    return pl.pallas_call(
        _t5_bias_kernel,
        grid=(2 * N_PAIR, seq // tq),
        in_specs=[pl.BlockSpec(memory_space=pltpu.SMEM), pl.BlockSpec(memory_space=pltpu.SMEM)],
        out_specs=pl.BlockSpec((1, tq, seq), lambda h, qi: (h, qi, 0)),
        out_shape=jax.ShapeDtypeStruct((2 * N_PAIR, seq, seq), F32),
        compiler_params=pltpu.CompilerParams(dimension_semantics=("parallel", "parallel"),
                                             vmem_limit_bytes=VMEM_LIMIT),
        name="t5_bias",
    )(thr, tab.reshape(-1))


def _swap_rope_lanes(w):
    r = MLA_ROPE // 2
    return jnp.concatenate([w[..., :MLA_NOPE], w[..., MLA_NOPE + r:], w[..., MLA_NOPE:MLA_NOPE + r]], axis=-1)


def _pad_lanes(w, width=LANES):
    return jnp.concatenate([w, jnp.zeros(w.shape[:-1] + (width - w.shape[-1],), w.dtype)], axis=-1)


def _block_ones(width, seg):
    idx = np.arange(width) // seg
    return jnp.asarray(idx[:, None] == idx[None, :], dtype=BF16)


def kernel(x, ffn1_norm, ffn1_w_gate, ffn1_w_up, ffn1_w_down, mix_norm, w_in, mla_q_norm, mla_w_uq, mla_kv_norm, mla_w_ukv, mla_qk_q_norm, mla_qk_k_norm, mla_out_norm, na_qk_q_norm, na_qk_k_norm, na_rpb, na_out_norm, diff_qk_q_norm, diff_qk_k_norm, diff_lambda_q1, diff_lambda_k1, diff_lambda_q2, diff_lambda_k2, diff_subln, t5_table, w_out, ffn2_norm, ffn2_w_gate, ffn2_w_up, ffn2_w_down):
    b, s, d = x.shape
    depth = w_in.shape[0]
    t = b * s
    tq = min(256, s)

    inv_freq = ROPE_BASE ** (-jnp.arange(0, MLA_ROPE, 2, dtype=F32) / MLA_ROPE)
    ang = jnp.arange(s, dtype=F32)[:, None] * inv_freq[None, :]
    cos, sin = jnp.cos(ang), jnp.sin(ang)
    cos_tab = jnp.concatenate([jnp.ones((s, MLA_NOPE), F32), cos, cos, jnp.zeros((s, LANES - MLA_QK), F32)], axis=-1)
    sin_tab = jnp.concatenate([jnp.zeros((s, MLA_NOPE), F32), -sin, sin, jnp.zeros((s, LANES - MLA_QK), F32)], axis=-1)

    t5_bias = _t5_bias(t5_table, s, tq)
    ones64 = _block_ones(W_SLOT, HEAD_DIM)
    ones32 = _block_ones(W_SLOT, DIFF_D)
    seg = H_NA * HEAD_DIM
    in_groups = [(0, Q_LORA + KV_LORA + MLA_ROPE, 2 * Q_LORA)] + [
        (Q_LORA + KV_LORA + MLA_ROPE + i * seg, seg, W_SLOT) for i in range(6)]

    h = x.reshape(t, d)
    for l in range(depth):
        row = lambda v: v[l][None, :]
        h = _ffn(h, row(ffn1_norm), ffn1_w_gate[l].astype(BF16), ffn1_w_up[l].astype(BF16),
                 ffn1_w_down[l].astype(BF16), s)

        win = _pad_cols(w_in[l], in_groups).astype(BF16)
        wuq = mla_w_uq[l].reshape(Q_LORA, H_MLA, MLA_QK)
        wq_all = jnp.concatenate([_pad_lanes(wuq).reshape(Q_LORA, -1),
                                  _pad_lanes(_swap_rope_lanes(wuq)).reshape(Q_LORA, -1)], axis=-1).astype(BF16)
        wukv = mla_w_ukv[l].reshape(KV_LORA, H_MLA, MLA_NOPE + HEAD_DIM)
        eye = jnp.broadcast_to(jnp.eye(MLA_ROPE, dtype=F32)[:, None, :], (MLA_ROPE, H_MLA, MLA_ROPE))
        k_top = jnp.concatenate([wukv[..., :MLA_NOPE], jnp.zeros((KV_LORA, H_MLA, MLA_ROPE), F32)], axis=-1)
        k_bot = jnp.concatenate([jnp.zeros((MLA_ROPE, H_MLA, MLA_NOPE), F32), eye], axis=-1)
        k_rows = jnp.concatenate([k_top, k_bot, jnp.zeros((KV_LORA - MLA_ROPE, H_MLA, MLA_QK), F32)], axis=0)
        v_rows = jnp.concatenate([wukv[..., MLA_NOPE:].reshape(KV_LORA, -1),
                                  jnp.zeros((KV_LORA, H_MLA * HEAD_DIM), F32)], axis=0)
        wk_all = jnp.concatenate([_pad_lanes(k_rows).reshape(2 * KV_LORA, -1),
                                  _pad_lanes(_swap_rope_lanes(k_rows)).reshape(2 * KV_LORA, -1),
                                  v_rows], axis=-1).astype(BF16)
        gq, gk = mla_qk_q_norm[l], mla_qk_k_norm[l]
        tile_pairs = lambda g, n: _pad_lanes(jnp.tile(g, n), W_SLOT)[None, :]
        consts_a = [row(mix_norm), win, row(mla_q_norm), wq_all, row(mla_kv_norm), wk_all,
                    _pad_lanes(gq)[None, :], _pad_lanes(_swap_rope_lanes(gq))[None, :],
                    _pad_lanes(gk)[None, :], _pad_lanes(_swap_rope_lanes(gk))[None, :]]
        consts_b = [ones64, ones32, tile_pairs(na_qk_q_norm[l], H_NA), tile_pairs(na_qk_k_norm[l], H_NA),
                    tile_pairs(diff_qk_q_norm[l], 2 * H_DIFF), tile_pairs(diff_qk_k_norm[l], 2 * H_DIFF)]
        qm, km, vm, nq, nk, nv, dq, dk, dv = _prep(h, consts_a, cos_tab, sin_tab, consts_b, s)
        seq3 = lambda a: a.reshape(b, s, a.shape[-1])

        o_mla = _mla_attn(seq3(qm), seq3(km), seq3(vm), tq)
        o_na = _na_attn(seq3(nq), seq3(nk), seq3(nv), _na_bias_table(na_rpb[l], s))
        lam_init = 0.8 - 0.6 * math.exp(-0.3 * l)
        lam = (jnp.exp(jnp.sum(diff_lambda_q1[l] * diff_lambda_k1[l]))
               - jnp.exp(jnp.sum(diff_lambda_q2[l] * diff_lambda_k2[l])) + lam_init).reshape(1)
        o_diff = _diff_attn(lam, seq3(dq), seq3(dk), seq3(dv), t5_bias, tq)

        wo = w_out[l]
        wo_all = jnp.concatenate([wo[:H_MLA * HEAD_DIM],
                                  wo[H_MLA * HEAD_DIM:H_MLA * HEAD_DIM + seg], jnp.zeros((W_SLOT - seg, d), F32),
                                  wo[H_MLA * HEAD_DIM + seg:], jnp.zeros((W_SLOT - seg, d), F32)], axis=0).astype(BF16)
        h = _outproj_ffn(h, o_mla.reshape(t, -1), o_na.reshape(t, -1), o_diff.reshape(t, -1),
                         row(mla_out_norm), _pad_lanes(na_out_norm[l], W_SLOT)[None, :],
                         tile_pairs(diff_subln[l], H_DIFF), ones64, wo_all,
                         row(ffn2_norm), ffn2_w_gate[l].astype(BF16), ffn2_w_up[l].astype(BF16),
                         ffn2_w_down[l].astype(BF16), s, 1.0 - lam_init)
    return h.reshape(b, s, d)
```

```python
import functools
import math

import numpy as np
import jax
import jax.numpy as jnp
from jax import lax
from jax.experimental import pallas as pl
from jax.experimental.pallas import tpu as pltpu

D_MODEL = 1024
HEAD_DIM = 64
H_MLA, H_NA, H_DIFF = 6, 5, 5
Q_LORA, KV_LORA = 256, 128
MLA_NOPE, MLA_ROPE = 64, 32
MLA_QK = MLA_NOPE + MLA_ROPE
ROPE_BASE = 10000.0
GRID_W = 64
NA_KR_MAX, NA_KC = 8, 16
DIFF_D = 32
T5_BUCKETS, T5_MAX_DIST = 32, 128
D_FF = 2816
EPS = 1e-6

LANES = 128
N_PAIR = 3
W_SLOT = N_PAIR * LANES
NP_IN = 512 + 6 * W_SLOT
NA_QROWS = 4
KEY_CHUNK = 512
VT_ROWS = HEAD_DIM + 16
NA_WROWS = 12
MASK_VALUE = -1e30
LOG2E = math.log2(math.e)
VMEM_LIMIT = 56 * 1024 * 1024

BF16 = jnp.bfloat16
F32 = jnp.float32


def _dot(a, b):
    return jnp.dot(a, b, preferred_element_type=F32)


def _dot_nt(a, b):
    return lax.dot_general(a, b, (((1,), (1,)), ((), ())), preferred_element_type=F32)


def _const_spec(shape):
    nd = len(shape)
    return pl.BlockSpec(shape, lambda *_: (0,) * nd, pipeline_mode=pl.Buffered(1))


def _swiglu_half(h, g_ref, wg_ref, wu_ref, wd_ref):
    ms = jnp.mean(h * h, axis=-1, keepdims=True)
    xn = (h * lax.rsqrt(ms + EPS) * g_ref[...]).astype(BF16)
    gate = _dot(xn, wg_ref[...])
    up = _dot(xn, wu_ref[...])
    act = (gate * (1.0 / (1.0 + jnp.exp(-gate))) * up).astype(BF16)
    return h + 0.5 * _dot(act, wd_ref[...])


def _ffn_kernel(h_ref, g_ref, wg_ref, wu_ref, wd_ref, o_ref):
    o_ref[...] = _swiglu_half(h_ref[...], g_ref, wg_ref, wu_ref, wd_ref)


def _seg_mean_sq(x, ones_ref, seg):
    return _dot((x * x).astype(BF16), ones_ref[...]) * (1.0 / seg)


def _outproj_ffn_kernel(h_ref, om_ref, on_ref, od_ref, gm_ref, gn_ref, gd_ref, ones64_ref, wo_ref,
                        g_ref, wg_ref, wu_ref, wd_ref, o_ref, *, diff_out_scale):
    om = om_ref[...].astype(F32)
    on = on_ref[...].astype(F32)
    od = od_ref[...].astype(F32)
    om = om * lax.rsqrt(jnp.sum(om * om, axis=-1, keepdims=True) * (1.0 / (H_MLA * HEAD_DIM)) + EPS) * gm_ref[...]
    on = on * lax.rsqrt(jnp.sum(on * on, axis=-1, keepdims=True) * (1.0 / (H_NA * HEAD_DIM)) + EPS) * gn_ref[...]
    od = od * lax.rsqrt(_seg_mean_sq(od, ones64_ref, HEAD_DIM) + EPS) * (gd_ref[...] * diff_out_scale)
    cat = jnp.concatenate([om, on, od], axis=-1).astype(BF16)
    h = h_ref[...] + _dot(cat, wo_ref[...])
    o_ref[...] = _swiglu_half(h, g_ref, wg_ref, wu_ref, wd_ref)


def _token_tile(n_tokens, seq):
    tm = min(512, seq)
    assert seq % tm == 0 and n_tokens % tm == 0
    return tm


def _ffn(h, g, wg, wu, wd, seq):
    t, d = h.shape
    tm = _token_tile(t, seq)
    row = pl.BlockSpec((tm, d), lambda i: (i, 0))
    return pl.pallas_call(
        _ffn_kernel,
        grid=(t // tm,),
        in_specs=[row, _const_spec(g.shape), _const_spec(wg.shape), _const_spec(wu.shape), _const_spec(wd.shape)],
        out_specs=row,
        out_shape=jax.ShapeDtypeStruct((t, d), F32),
        compiler_params=pltpu.CompilerParams(dimension_semantics=("parallel",), vmem_limit_bytes=VMEM_LIMIT),
        name="ffn",
    )(h, g, wg, wu, wd)


def _outproj_ffn(h, om, on, od, gm, gn, gd, ones64, wo, g, wg, wu, wd, seq, diff_out_scale):
    t, d = h.shape
    tm = _token_tile(t, seq) // 2
    row = pl.BlockSpec((tm, d), lambda i: (i, 0))
    slot = pl.BlockSpec((tm, W_SLOT), lambda i: (i, 0))
    consts = [gm, gn, gd, ones64, wo, g, wg, wu, wd]
    return pl.pallas_call(
        functools.partial(_outproj_ffn_kernel, diff_out_scale=diff_out_scale),
        grid=(t // tm,),
        in_specs=[row, slot, slot, slot] + [_const_spec(c.shape) for c in consts],
        out_specs=row,
        out_shape=jax.ShapeDtypeStruct((t, d), F32),
        compiler_params=pltpu.CompilerParams(dimension_semantics=("parallel",), vmem_limit_bytes=VMEM_LIMIT),
        name="outproj_ffn",
    )(h, om, on, od, *consts)


def _prep_kernel(h_ref, gmix_ref, win_ref, gq_ref, wq_ref, gkv_ref, wk_ref,
                 gqc_ref, gqs_ref, gkc_ref, gks_ref, cos_ref, sin_ref,
                 ones64_ref, ones32_ref, gnq_ref, gnk_ref, gdq_ref, gdk_ref,
                 qm_ref, km_ref, vm_ref, nq_ref, nk_ref, nv_ref, dq_ref, dk_ref, dv_ref):
    h = h_ref[...]
    ms = jnp.mean(h * h, axis=-1, keepdims=True)
    xn = (h * lax.rsqrt(ms + EPS) * gmix_ref[...]).astype(BF16)
    u = _dot(xn, win_ref[...])

    cq = u[:, 0:Q_LORA]
    cqn = (cq * lax.rsqrt(jnp.mean(cq * cq, axis=-1, keepdims=True) + EPS) * gq_ref[...]).astype(BF16)
    q2 = _dot(cqn, wq_ref[...])
    ckv = u[:, Q_LORA:Q_LORA + KV_LORA]
    ckvn = ckv * lax.rsqrt(jnp.mean(ckv * ckv, axis=-1, keepdims=True) + EPS) * gkv_ref[...]
    kr = u[:, Q_LORA + KV_LORA:Q_LORA + 2 * KV_LORA]
    k2 = _dot(jnp.concatenate([ckvn, kr], axis=-1).astype(BF16), wk_ref[...])

    cos = cos_ref[...]
    sin = sin_ref[...]
    gqc = gqc_ref[...] * cos
    gqs = gqs_ref[...] * sin
    gkc = gkc_ref[...] * cos
    gks = gks_ref[...] * sin
    nsw = H_MLA * LANES
    for hd in range(H_MLA):
        lo, hi = hd * LANES, (hd + 1) * LANES
        qr, qs = q2[:, lo:hi], q2[:, nsw + lo:nsw + hi]
        inv = lax.rsqrt(jnp.sum(qr * qr, axis=-1, keepdims=True) * (1.0 / MLA_QK) + EPS) * (MLA_QK ** -0.5 * LOG2E)
        qm_ref[:, lo:hi] = ((qr * gqc + qs * gqs) * inv).astype(BF16)
        kr_, ks_ = k2[:, lo:hi], k2[:, nsw + lo:nsw + hi]
        inv = lax.rsqrt(jnp.sum(kr_ * kr_, axis=-1, keepdims=True) * (1.0 / MLA_QK) + EPS)
        km_ref[:, lo:hi] = ((kr_ * gkc + ks_ * gks) * inv).astype(BF16)
    vm_ref[...] = k2[:, 2 * nsw:].astype(BF16)

    base = 2 * Q_LORA
    naq = u[:, base:base + W_SLOT]
    nak = u[:, base + W_SLOT:base + 2 * W_SLOT]
    nq_ref[...] = (naq * lax.rsqrt(_seg_mean_sq(naq, ones64_ref, HEAD_DIM) + EPS)
                   * (gnq_ref[...] * (HEAD_DIM ** -0.5 * LOG2E))).astype(BF16)
    nk_ref[...] = (nak * lax.rsqrt(_seg_mean_sq(nak, ones64_ref, HEAD_DIM) + EPS) * gnk_ref[...]).astype(BF16)
    nv_ref[...] = u[:, base + 2 * W_SLOT:base + 3 * W_SLOT].astype(BF16)
    base += 3 * W_SLOT
    dfq = u[:, base:base + W_SLOT]
    dfk = u[:, base + W_SLOT:base + 2 * W_SLOT]
    dq_ref[...] = (dfq * lax.rsqrt(_seg_mean_sq(dfq, ones32_ref, DIFF_D) + EPS)
                   * (gdq_ref[...] * (DIFF_D ** -0.5 * LOG2E))).astype(BF16)
    dk_ref[...] = (dfk * lax.rsqrt(_seg_mean_sq(dfk, ones32_ref, DIFF_D) + EPS) * gdk_ref[...]).astype(BF16)
    dv_ref[...] = u[:, base + 2 * W_SLOT:base + 3 * W_SLOT].astype(BF16)


def _prep(h, consts_a, cos, sin, consts_b, seq):
    t, d = h.shape
    tm = _token_tile(t, seq)
    tiles_per_seq = seq // tm
    row = pl.BlockSpec((tm, d), lambda i: (i, 0))
    pos = pl.BlockSpec((tm, LANES), lambda i: (i % tiles_per_seq, 0))
    widths = [H_MLA * LANES, H_MLA * LANES] + [W_SLOT] * 7
    return pl.pallas_call(
        _prep_kernel,
        grid=(t // tm,),
        in_specs=[row] + [_const_spec(c.shape) for c in consts_a] + [pos, pos]
        + [_const_spec(c.shape) for c in consts_b],
        out_specs=[pl.BlockSpec((tm, w), lambda i: (i, 0)) for w in widths],
        out_shape=[jax.ShapeDtypeStruct((t, w), BF16) for w in widths],
        compiler_params=pltpu.CompilerParams(dimension_semantics=("parallel",), vmem_limit_bytes=VMEM_LIMIT),
        name="mix_prep",
    )(h, *consts_a, cos, sin, *consts_b)


def _softmax_cols(s):
    m = jnp.max(s, axis=0, keepdims=True)
    p = jnp.exp2(s - m)
    return p, 1.0 / jnp.sum(p, axis=0, keepdims=True)


def _lane_band(x, lo, width):
    lane = lax.broadcasted_iota(jnp.int32, x.shape, 1)
    return jnp.where((lane >= lo) & (lane < lo + width), x, jnp.zeros_like(x))


def _merge_heads_t(outs):
    rows = lax.broadcasted_iota(jnp.int32, outs[0].shape, 0)
    return jnp.where(rows < HEAD_DIM, outs[0], outs[1]).T


def _pipelined_softmax_pv(score_fns, vt_fns, s_bufs, n_keys):
    n_chunk = n_keys // KEY_CHUNK
    rows = lambda c: slice(c * KEY_CHUNK, (c + 1) * KEY_CHUNK)

    def score_pass(u, c, m):
        s = score_fns[u](c)
        s_bufs[u % 2][rows(c), :] = s
        cm = jnp.max(s, axis=0, keepdims=True)
        return cm if m is None else jnp.maximum(m, cm)

    m = None
    for c in range(n_chunk):
        m = score_pass(0, c, m)
    results = []
    for u in range(len(score_fns)):
        m_next = acc = l = None
        for c in range(n_chunk):
            if u + 1 < len(score_fns):
                m_next = score_pass(u + 1, c, m_next)
            p = jnp.exp2(s_bufs[u % 2][rows(c), :] - m)
            vt = vt_fns[u](c)
            ca = _dot(vt, p.astype(BF16))
            acc = ca if acc is None else acc + ca
            if vt.shape[0] == HEAD_DIM:
                cl = jnp.sum(p, axis=0, keepdims=True)
                l = cl if l is None else l + cl
        results.append((acc[:HEAD_DIM], acc[HEAD_DIM:HEAD_DIM + 1] if l is None else l))
        m = m_next
    return results


def _store_head_pairs(o_ref, heads_t):
    if len(heads_t) % 2:
        heads_t = heads_t + [jnp.zeros_like(heads_t[0])]
    for p in range(len(heads_t) // 2):
        pair_t = jnp.concatenate(heads_t[2 * p:2 * p + 2], axis=0)
        o_ref[0, :, p * LANES:(p + 1) * LANES] = pair_t.T.astype(o_ref.dtype)


def _mla_attn_kernel(q_ref, k_ref, vt_ref, o_ref, s0_ref, s1_ref):
    n_keys = k_ref.shape[1]
    head_cols = lambda h: slice(h * LANES, (h + 1) * LANES)
    score_fns = [lambda c, h=h: _dot_nt(k_ref[0, c * KEY_CHUNK:(c + 1) * KEY_CHUNK, head_cols(h)],
                                        q_ref[0, :, head_cols(h)]) for h in range(H_MLA)]
    vt_fns = [lambda c, h=h: vt_ref[0, h * HEAD_DIM:(h + 1) * HEAD_DIM, c * KEY_CHUNK:(c + 1) * KEY_CHUNK]
              for h in range(H_MLA)]
    res = _pipelined_softmax_pv(score_fns, vt_fns, (s0_ref, s1_ref), n_keys)
    _store_head_pairs(o_ref, [acc * (1.0 / l) for acc, l in res])


def _mla_attn(q, k, vt, tq):
    b, s, _ = q.shape
    return pl.pallas_call(
        _mla_attn_kernel,
        grid=(b, s // tq),
        in_specs=[pl.BlockSpec((1, tq, H_MLA * LANES), lambda bi, qi: (bi, qi, 0)),
                  pl.BlockSpec((1, s, H_MLA * LANES), lambda bi, qi: (bi, 0, 0)),
                  pl.BlockSpec((1, W_SLOT, s), lambda bi, qi: (bi, 0, 0))],
        out_specs=pl.BlockSpec((1, tq, W_SLOT), lambda bi, qi: (bi, qi, 0)),
        out_shape=jax.ShapeDtypeStruct((b, s, W_SLOT), BF16),
        scratch_shapes=[pltpu.VMEM((s, tq), F32), pltpu.VMEM((s, tq), F32)],
        compiler_params=pltpu.CompilerParams(dimension_semantics=("parallel", "parallel"),
                                             vmem_limit_bytes=VMEM_LIMIT),
        name="mla_attn",
    )(q, k, vt)


def _diff_attn_kernel(lam_ref, q_ref, k_ref, vt_ref, bias_ref, o_ref, s0_ref, s1_ref):
    lam = lam_ref[0]
    n_keys = k_ref.shape[1]
    key_rows = lambda c: slice(c * KEY_CHUNK, (c + 1) * KEY_CHUNK)
    score_fns, vt_fns = [], []
    for h in range(H_DIFF):
        slot = slice((h // 2) * LANES, (h // 2 + 1) * LANES)
        for m in range(2):
            qm = _lane_band(q_ref[0, :, slot], (2 * (h % 2) + m) * DIFF_D, DIFF_D)
            score_fns.append(lambda c, h=h, slot=slot, qm=qm:
                             _dot_nt(k_ref[0, key_rows(c), slot], qm) + bias_ref[h, key_rows(c), :])
            vt_fns.append(lambda c, h=h: vt_ref[0, h * VT_ROWS:(h + 1) * VT_ROWS, key_rows(c)])
    res = _pipelined_softmax_pv(score_fns, vt_fns, (s0_ref, s1_ref), n_keys)
    heads_t = []
    for h in range(H_DIFF):
        (a1, l1), (a2, l2) = res[2 * h], res[2 * h + 1]
        heads_t.append(a1 * (1.0 / l1) - a2 * (lam / l2))
    _store_head_pairs(o_ref, heads_t)


def _diff_attn(lam, q, k, vt, bias_t, tq):
    b, s, _ = q.shape
    return pl.pallas_call(
        _diff_attn_kernel,
        grid=(s // tq, b),
        in_specs=[pl.BlockSpec(memory_space=pltpu.SMEM),
                  pl.BlockSpec((1, tq, W_SLOT), lambda qi, bi: (bi, qi, 0)),
                  pl.BlockSpec((1, s, W_SLOT), lambda qi, bi: (bi, 0, 0)),
                  pl.BlockSpec((1, H_DIFF * VT_ROWS, s), lambda qi, bi: (bi, 0, 0)),
                  pl.BlockSpec((H_DIFF, s, tq), lambda qi, bi: (0, 0, qi), pipeline_mode=pl.Buffered(1))],
        out_specs=pl.BlockSpec((1, tq, W_SLOT), lambda qi, bi: (bi, qi, 0)),
        out_shape=jax.ShapeDtypeStruct((b, s, W_SLOT), BF16),
        scratch_shapes=[pltpu.VMEM((s, tq), F32), pltpu.VMEM((s, tq), F32)],
        compiler_params=pltpu.CompilerParams(dimension_semantics=("parallel", "parallel"),
                                             vmem_limit_bytes=VMEM_LIMIT),
        name="diff_attn",
    )(lam, q, k, vt, bias_t)


def _na_plan(seq):
    rows = seq // GRID_W
    kr = min(NA_KR_MAX, rows)
    assert rows % NA_QROWS == 0 and rows >= NA_WROWS
    patterns, blocks = [], []
    for r0 in range(0, rows, NA_QROWS):
        ws = int(np.clip(r0 - kr // 2, 0, rows - NA_WROWS))
        rs = [int(np.clip(r0 + j - kr // 2, 0, rows - kr)) for j in range(NA_QROWS)]
        assert all(ws <= x and x + kr <= ws + NA_WROWS for x in rs)
        pat = (r0 - ws, tuple(x - ws for x in rs))
        if pat not in patterns:
            patterns.append(pat)
        blocks.append((r0, ws, patterns.index(pat)))
    return kr, patterns, blocks


def _na_bias_table(rpb, seq):
    kr, patterns, _ = _na_plan(seq)
    nq, nk = NA_QROWS * GRID_W, NA_WROWS * GRID_W
    c = np.arange(GRID_W)
    cstart = np.clip(c - NA_KC // 2, 0, GRID_W - NA_KC)
    col_ok = (c[None, :] >= cstart[:, None]) & (c[None, :] < cstart[:, None] + NA_KC)
    col_sel = (c[None, :, None] - c[:, None, None] + NA_KC - 1) == np.arange(2 * NA_KC - 1)
    j, i = np.arange(NA_QROWS), np.arange(NA_WROWS)
    row_ok, row_sel = [], []
    for d0, rs_rel in patterns:
        rs = np.asarray(rs_rel)
        row_ok.append((i[None, :] >= rs[:, None]) & (i[None, :] < rs[:, None] + kr))
        row_sel.append((i[None, :, None] - (d0 + j[:, None, None]) + NA_KR_MAX - 1) == np.arange(2 * NA_KR_MAX - 1))
    row_ok, row_sel = np.stack(row_ok), np.stack(row_sel)
    rows = jnp.einsum('pjia,hab->hpjib', jnp.asarray(row_sel, F32), rpb * LOG2E, precision=lax.Precision.HIGHEST)
    vals = jnp.einsum('hpjib,ckb->hpikjc', rows, jnp.asarray(col_sel, F32), precision=lax.Precision.HIGHEST)
    ok = row_ok.transpose(0, 2, 1)[:, :, None, :, None] & col_ok.T[None, None, :, None, :]
    tab = jnp.where(jnp.asarray(ok)[None], vals, MASK_VALUE).reshape(H_NA, len(patterns), nk, nq)
    tab = jnp.concatenate([tab, jnp.zeros((2 * N_PAIR - H_NA,) + tab.shape[1:], F32)], axis=0)
    return tab.reshape(N_PAIR, 2, len(patterns), nk, nq)


def _na_attn_kernel(q_ref, k_ref, vt_ref, bias_ref, o_ref, *, blocks):
    nq, nk = NA_QROWS * GRID_W, NA_WROWS * GRID_W
    for r0, ws, pat in blocks:
        q = q_ref[0, r0 * GRID_W:r0 * GRID_W + nq, :]
        k = k_ref[0, ws * GRID_W:ws * GRID_W + nk, :]
        vt = vt_ref[0, :, ws * GRID_W:ws * GRID_W + nk]
        outs = []
        for j in range(2):
            qj = _lane_band(q, j * HEAD_DIM, HEAD_DIM)
            p, inv_l = _softmax_cols(_dot_nt(k, qj) + bias_ref[0, j, pat])
            outs.append(_dot(vt, p.astype(BF16)) * inv_l)
        o_ref[0, r0 * GRID_W:r0 * GRID_W + nq, :] = _merge_heads_t(outs).astype(o_ref.dtype)


def _na_attn(q, k, vt, bias_tab):
    b, s, _ = q.shape
    _, _, blocks = _na_plan(s)
    tok = pl.BlockSpec((1, s, LANES), lambda p, bi: (bi, 0, p))
    return pl.pallas_call(
        functools.partial(_na_attn_kernel, blocks=tuple(blocks)),
        grid=(N_PAIR, b),
        in_specs=[tok, tok, pl.BlockSpec((1, LANES, s), lambda p, bi: (bi, p, 0)),
                  pl.BlockSpec((1,) + bias_tab.shape[1:], lambda p, bi: (p, 0, 0, 0, 0))],
        out_specs=tok,
        out_shape=jax.ShapeDtypeStruct((b, s, W_SLOT), BF16),
        compiler_params=pltpu.CompilerParams(dimension_semantics=("parallel", "parallel"),
                                             vmem_limit_bytes=VMEM_LIMIT),
        name="na_attn",
    )(q, k, vt, bias_tab)


def _pad_cols(w, groups):
    parts = []
    for start, width, padded in groups:
        parts.append(w[..., start:start + width])
        if padded > width:
            parts.append(jnp.zeros(w.shape[:-1] + (padded - width,), w.dtype))
    return jnp.concatenate(parts, axis=-1)


def _t5_bucket(rel):
    half = T5_BUCKETS // 2
    max_exact = half // 2
    n = jnp.abs(rel)
    nf = jnp.maximum(n, 1).astype(F32)
    large = max_exact + (jnp.log(nf / max_exact) / math.log(T5_MAX_DIST / max_exact)
                         * (half - max_exact)).astype(jnp.int32)
    large = jnp.minimum(large, half - 1)
    return jnp.where(rel > 0, half, 0) + jnp.where(n < max_exact, n, large)


def _t5_bias_kernel(thr_ref, tab_ref, o_ref):
    half = T5_BUCKETS // 2
    head = pl.program_id(0)
    tk, s = o_ref.shape[1:]
    k = pl.program_id(1) * tk + lax.broadcasted_iota(jnp.int32, (tk, s), 0)
    rel = k - lax.broadcasted_iota(jnp.int32, (tk, s), 1)
    n = jnp.abs(rel)
    neg = jnp.full((tk, s), tab_ref[head * T5_BUCKETS], F32)
    pos = jnp.full((tk, s), tab_ref[head * T5_BUCKETS + half], F32)
    for j in range(1, half):
        ge = n >= thr_ref[j]
        neg = jnp.where(ge, tab_ref[head * T5_BUCKETS + j], neg)
        pos = jnp.where(ge, tab_ref[head * T5_BUCKETS + half + j], pos)
    o_ref[0] = jnp.where(rel > 0, pos, neg)


def _t5_bias(t5_table, seq, tq):
    half = T5_BUCKETS // 2
    dist_bucket = _t5_bucket(-jnp.arange(seq, dtype=jnp.int32))
    thr = jnp.sum(dist_bucket[None, :] < jnp.arange(half, dtype=jnp.int32)[:, None], axis=1).astype(jnp.int32)
    tab = t5_table.T * LOG2E
    return pl.pallas_call(
        _t5_bias_kernel,
        grid=(H_DIFF, seq // tq),
        in_specs=[pl.BlockSpec(memory_space=pltpu.SMEM), pl.BlockSpec(memory_space=pltpu.SMEM)],
        out_specs=pl.BlockSpec((1, tq, seq), lambda h, ki: (h, ki, 0)),
        out_shape=jax.ShapeDtypeStruct((H_DIFF, seq, seq), F32),
        compiler_params=pltpu.CompilerParams(dimension_semantics=("parallel", "parallel"),
                                             vmem_limit_bytes=VMEM_LIMIT),
        name="t5_bias",
    )(thr, tab.reshape(-1))


def _swap_rope_lanes(w):
    r = MLA_ROPE // 2
    return jnp.concatenate([w[..., :MLA_NOPE], w[..., MLA_NOPE + r:], w[..., MLA_NOPE:MLA_NOPE + r]], axis=-1)


def _pad_lanes(w, width=LANES):
    return jnp.concatenate([w, jnp.zeros(w.shape[:-1] + (width - w.shape[-1],), w.dtype)], axis=-1)


def _block_ones(width, seg):
    idx = np.arange(width) // seg
    return jnp.asarray(idx[:, None] == idx[None, :], dtype=BF16)


def kernel(x, ffn1_norm, ffn1_w_gate, ffn1_w_up, ffn1_w_down, mix_norm, w_in, mla_q_norm, mla_w_uq, mla_kv_norm, mla_w_ukv, mla_qk_q_norm, mla_qk_k_norm, mla_out_norm, na_qk_q_norm, na_qk_k_norm, na_rpb, na_out_norm, diff_qk_q_norm, diff_qk_k_norm, diff_lambda_q1, diff_lambda_k1, diff_lambda_q2, diff_lambda_k2, diff_subln, t5_table, w_out, ffn2_norm, ffn2_w_gate, ffn2_w_up, ffn2_w_down):
    b, s, d = x.shape
    depth = w_in.shape[0]
    t = b * s
    tq = min(512, s)

    inv_freq = ROPE_BASE ** (-jnp.arange(0, MLA_ROPE, 2, dtype=F32) / MLA_ROPE)
    ang = jnp.arange(s, dtype=F32)[:, None] * inv_freq[None, :]
    cos, sin = jnp.cos(ang), jnp.sin(ang)
    cos_tab = jnp.concatenate([jnp.ones((s, MLA_NOPE), F32), cos, cos, jnp.zeros((s, LANES - MLA_QK), F32)], axis=-1)
    sin_tab = jnp.concatenate([jnp.zeros((s, MLA_NOPE), F32), -sin, sin, jnp.zeros((s, LANES - MLA_QK), F32)], axis=-1)

    t5_bias = _t5_bias(t5_table, s, tq)
    ones64 = _block_ones(W_SLOT, HEAD_DIM)
    ones32 = _block_ones(W_SLOT, DIFF_D)
    seg = H_NA * HEAD_DIM
    in_groups = [(0, Q_LORA + KV_LORA + MLA_ROPE, 2 * Q_LORA)] + [
        (Q_LORA + KV_LORA + MLA_ROPE + i * seg, seg, W_SLOT) for i in range(6)]

    h = x.reshape(t, d)
    for l in range(depth):
        row = lambda v: v[l][None, :]
        h = _ffn(h, row(ffn1_norm), ffn1_w_gate[l].astype(BF16), ffn1_w_up[l].astype(BF16),
                 ffn1_w_down[l].astype(BF16), s)

        win = _pad_cols(w_in[l], in_groups).astype(BF16)
        wuq = mla_w_uq[l].reshape(Q_LORA, H_MLA, MLA_QK)
        wq_all = jnp.concatenate([_pad_lanes(wuq).reshape(Q_LORA, -1),
                                  _pad_lanes(_swap_rope_lanes(wuq)).reshape(Q_LORA, -1)], axis=-1).astype(BF16)
        wukv = mla_w_ukv[l].reshape(KV_LORA, H_MLA, MLA_NOPE + HEAD_DIM)
        eye = jnp.broadcast_to(jnp.eye(MLA_ROPE, dtype=F32)[:, None, :], (MLA_ROPE, H_MLA, MLA_ROPE))
        k_top = jnp.concatenate([wukv[..., :MLA_NOPE], jnp.zeros((KV_LORA, H_MLA, MLA_ROPE), F32)], axis=-1)
        k_bot = jnp.concatenate([jnp.zeros((MLA_ROPE, H_MLA, MLA_NOPE), F32), eye], axis=-1)
        k_rows = jnp.concatenate([k_top, k_bot, jnp.zeros((KV_LORA - MLA_ROPE, H_MLA, MLA_QK), F32)], axis=0)
        v_rows = jnp.concatenate([wukv[..., MLA_NOPE:].reshape(KV_LORA, -1),
                                  jnp.zeros((KV_LORA, H_MLA * HEAD_DIM), F32)], axis=0)
        wk_all = jnp.concatenate([_pad_lanes(k_rows).reshape(2 * KV_LORA, -1),
                                  _pad_lanes(_swap_rope_lanes(k_rows)).reshape(2 * KV_LORA, -1),
                                  v_rows], axis=-1).astype(BF16)
        gq, gk = mla_qk_q_norm[l], mla_qk_k_norm[l]
        tile_pairs = lambda g, n: _pad_lanes(jnp.tile(g, n), W_SLOT)[None, :]
        consts_a = [row(mix_norm), win, row(mla_q_norm), wq_all, row(mla_kv_norm), wk_all,
                    _pad_lanes(gq)[None, :], _pad_lanes(_swap_rope_lanes(gq))[None, :],
                    _pad_lanes(gk)[None, :], _pad_lanes(_swap_rope_lanes(gk))[None, :]]
        consts_b = [ones64, ones32, tile_pairs(na_qk_q_norm[l], H_NA), tile_pairs(na_qk_k_norm[l], H_NA),
                    tile_pairs(diff_qk_q_norm[l], 2 * H_DIFF), tile_pairs(diff_qk_k_norm[l], 2 * H_DIFF)]
        qm, km, vm, nq, nk, nv, dq, dk, dv = _prep(h, consts_a, cos_tab, sin_tab, consts_b, s)
        seq3 = lambda a: a.reshape(b, s, a.shape[-1])

        seq3_t = lambda a: jnp.swapaxes(seq3(a), 1, 2)

        def heads_t_with_ones(v, n_heads):
            vh = v[:, :n_heads * HEAD_DIM].reshape(b, s, n_heads, HEAD_DIM)
            vh = jnp.concatenate([vh, jnp.ones((b, s, n_heads, VT_ROWS - HEAD_DIM), BF16)], axis=-1)
            return jnp.swapaxes(vh.reshape(b, s, n_heads * VT_ROWS), 1, 2)

        o_mla = _mla_attn(seq3(qm), seq3(km), seq3_t(vm), tq)
        o_na = _na_attn(seq3(nq), seq3(nk), seq3_t(nv), _na_bias_table(na_rpb[l], s))
        lam_init = 0.8 - 0.6 * math.exp(-0.3 * l)
        lam = (jnp.exp(jnp.sum(diff_lambda_q1[l] * diff_lambda_k1[l]))
               - jnp.exp(jnp.sum(diff_lambda_q2[l] * diff_lambda_k2[l])) + lam_init).reshape(1)
        o_diff = _diff_attn(lam, seq3(dq), seq3(dk), heads_t_with_ones(dv, H_DIFF), t5_bias, tq)

        wo = w_out[l]
        wo_all = jnp.concatenate([wo[:H_MLA * HEAD_DIM],
                                  wo[H_MLA * HEAD_DIM:H_MLA * HEAD_DIM + seg], jnp.zeros((W_SLOT - seg, d), F32),
                                  wo[H_MLA * HEAD_DIM + seg:], jnp.zeros((W_SLOT - seg, d), F32)], axis=0).astype(BF16)
        h = _outproj_ffn(h, o_mla.reshape(t, -1), o_na.reshape(t, -1), o_diff.reshape(t, -1),
                         row(mla_out_norm), _pad_lanes(na_out_norm[l], W_SLOT)[None, :],
                         tile_pairs(diff_subln[l], H_DIFF), ones64, wo_all,
                         row(ffn2_norm), ffn2_w_gate[l].astype(BF16), ffn2_w_up[l].astype(BF16),
                         ffn2_w_down[l].astype(BF16), s, 1.0 - lam_init)
    return h.reshape(b, s, d)
```

```python
import functools
import math

import numpy as np
import jax
import jax.numpy as jnp
from jax import lax
from jax.experimental import pallas as pl
from jax.experimental.pallas import tpu as pltpu

D_MODEL = 1024
HEAD_DIM = 64
H_MLA, H_NA, H_DIFF = 6, 5, 5
Q_LORA, KV_LORA = 256, 128
MLA_NOPE, MLA_ROPE = 64, 32
MLA_QK = MLA_NOPE + MLA_ROPE
ROPE_BASE = 10000.0
GRID_W = 64
NA_KR_MAX, NA_KC = 8, 16
DIFF_D = 32
T5_BUCKETS, T5_MAX_DIST = 32, 128
D_FF = 2816
EPS = 1e-6

LANES = 128
N_PAIR = 3
W_SLOT = N_PAIR * LANES
NP_IN = 512 + 6 * W_SLOT
NA_QROWS = 4
KEY_CHUNK = 1024
VT_ROWS = HEAD_DIM + 16
NA_VT_ROWS = LANES + 16
NA_WROWS = 12
MASK_VALUE = -1e30
LOG2E = math.log2(math.e)
VMEM_LIMIT = 56 * 1024 * 1024

BF16 = jnp.bfloat16
F32 = jnp.float32


def _dot(a, b):
    return jnp.dot(a, b, preferred_element_type=F32)


def _dot_nt(a, b):
    return lax.dot_general(a, b, (((1,), (1,)), ((), ())), preferred_element_type=F32)


def _const_spec(shape):
    nd = len(shape)
    return pl.BlockSpec(shape, lambda *_: (0,) * nd, pipeline_mode=pl.Buffered(1))


def _swiglu_half(h, g_ref, wg_ref, wu_ref, wd_ref):
    ms = jnp.mean(h * h, axis=-1, keepdims=True)
    xn = (h * lax.rsqrt(ms + EPS) * g_ref[...]).astype(BF16)
    gate = _dot(xn, wg_ref[...])
    up = _dot(xn, wu_ref[...])
    act = (gate * (1.0 / (1.0 + jnp.exp(-gate))) * up).astype(BF16)
    return h + 0.5 * _dot(act, wd_ref[...])


def _ffn_kernel(h_ref, g_ref, wg_ref, wu_ref, wd_ref, o_ref):
    o_ref[...] = _swiglu_half(h_ref[...], g_ref, wg_ref, wu_ref, wd_ref)


def _seg_mean_sq(x, ones_ref, seg):
    return _dot((x * x).astype(BF16), ones_ref[...]) * (1.0 / seg)


def _outproj_ffn_kernel(h_ref, om_ref, on_ref, od_ref, gm_ref, gn_ref, gd_ref, ones64_ref, wo_ref,
                        g_ref, wg_ref, wu_ref, wd_ref, o_ref, *, diff_out_scale):
    om = om_ref[...].astype(F32)
    on = on_ref[...].astype(F32)
    od = od_ref[...].astype(F32)
    om = om * lax.rsqrt(jnp.sum(om * om, axis=-1, keepdims=True) * (1.0 / (H_MLA * HEAD_DIM)) + EPS) * gm_ref[...]
    on = on * lax.rsqrt(jnp.sum(on * on, axis=-1, keepdims=True) * (1.0 / (H_NA * HEAD_DIM)) + EPS) * gn_ref[...]
    od = od * lax.rsqrt(_seg_mean_sq(od, ones64_ref, HEAD_DIM) + EPS) * (gd_ref[...] * diff_out_scale)
    cat = jnp.concatenate([om, on, od], axis=-1).astype(BF16)
    h = h_ref[...] + _dot(cat, wo_ref[...])
    o_ref[...] = _swiglu_half(h, g_ref, wg_ref, wu_ref, wd_ref)


def _token_tile(n_tokens, seq):
    tm = min(512, seq)
    assert seq % tm == 0 and n_tokens % tm == 0
    return tm


def _ffn(h, g, wg, wu, wd, seq):
    t, d = h.shape
    tm = _token_tile(t, seq)
    row = pl.BlockSpec((tm, d), lambda i: (i, 0))
    return pl.pallas_call(
        _ffn_kernel,
        grid=(t // tm,),
        in_specs=[row, _const_spec(g.shape), _const_spec(wg.shape), _const_spec(wu.shape), _const_spec(wd.shape)],
        out_specs=row,
        out_shape=jax.ShapeDtypeStruct((t, d), F32),
        compiler_params=pltpu.CompilerParams(dimension_semantics=("parallel",), vmem_limit_bytes=VMEM_LIMIT),
        name="ffn",
    )(h, g, wg, wu, wd)


def _outproj_ffn(h, om, on, od, gm, gn, gd, ones64, wo, g, wg, wu, wd, seq, diff_out_scale):
    t, d = h.shape
    tm = _token_tile(t, seq)
    row = pl.BlockSpec((tm, d), lambda i: (i, 0))
    slot = pl.BlockSpec((tm, W_SLOT), lambda i: (i, 0))
    consts = [gm, gn, gd, ones64, wo, g, wg, wu, wd]
    return pl.pallas_call(
        functools.partial(_outproj_ffn_kernel, diff_out_scale=diff_out_scale),
        grid=(t // tm,),
        in_specs=[row, slot, slot, slot] + [_const_spec(c.shape) for c in consts],
        out_specs=row,
        out_shape=jax.ShapeDtypeStruct((t, d), F32),
        compiler_params=pltpu.CompilerParams(dimension_semantics=("parallel",), vmem_limit_bytes=VMEM_LIMIT),
        name="outproj_ffn",
    )(h, om, on, od, *consts)


def _prep_kernel(h_ref, gmix_ref, win_ref, gq_ref, wq_ref, gkv_ref, wk_ref,
                 gqc_ref, gqs_ref, gkc_ref, gks_ref, cos_ref, sin_ref,
                 ones64_ref, ones32_ref, gnq_ref, gnk_ref, gdq_ref, gdk_ref,
                 qm_ref, km_ref, vm_ref, nq_ref, nk_ref, nv_ref, dq_ref, dk_ref, dv_ref):
    h = h_ref[...]
    ms = jnp.mean(h * h, axis=-1, keepdims=True)
    xn = (h * lax.rsqrt(ms + EPS) * gmix_ref[...]).astype(BF16)
    u = _dot(xn, win_ref[...])

    cq = u[:, 0:Q_LORA]
    cqn = (cq * lax.rsqrt(jnp.mean(cq * cq, axis=-1, keepdims=True) + EPS) * gq_ref[...]).astype(BF16)
    q2 = _dot(cqn, wq_ref[...])
    ckv = u[:, Q_LORA:Q_LORA + KV_LORA]
    ckvn = ckv * lax.rsqrt(jnp.mean(ckv * ckv, axis=-1, keepdims=True) + EPS) * gkv_ref[...]
    kr = u[:, Q_LORA + KV_LORA:Q_LORA + 2 * KV_LORA]
    k2 = _dot(jnp.concatenate([ckvn, kr], axis=-1).astype(BF16), wk_ref[...])

    cos = cos_ref[...]
    sin = sin_ref[...]
    gqc = gqc_ref[...] * cos
    gqs = gqs_ref[...] * sin
    gkc = gkc_ref[...] * cos
    gks = gks_ref[...] * sin
    nsw = H_MLA * LANES
    for hd in range(H_MLA):
        lo, hi = hd * LANES, (hd + 1) * LANES
        qr, qs = q2[:, lo:hi], q2[:, nsw + lo:nsw + hi]
        inv = lax.rsqrt(jnp.sum(qr * qr, axis=-1, keepdims=True) * (1.0 / MLA_QK) + EPS) * (MLA_QK ** -0.5 * LOG2E)
        qm_ref[:, lo:hi] = ((qr * gqc + qs * gqs) * inv).astype(BF16)
        kr_, ks_ = k2[:, lo:hi], k2[:, nsw + lo:nsw + hi]
        inv = lax.rsqrt(jnp.sum(kr_ * kr_, axis=-1, keepdims=True) * (1.0 / MLA_QK) + EPS)
        km_ref[:, lo:hi] = ((kr_ * gkc + ks_ * gks) * inv).astype(BF16)
    vm_ref[...] = k2[:, 2 * nsw:].astype(BF16)

    base = 2 * Q_LORA
    naq = u[:, base:base + W_SLOT]
    nak = u[:, base + W_SLOT:base + 2 * W_SLOT]
    nq_ref[...] = (naq * lax.rsqrt(_seg_mean_sq(naq, ones64_ref, HEAD_DIM) + EPS)
                   * (gnq_ref[...] * (HEAD_DIM ** -0.5 * LOG2E))).astype(BF16)
    nk_ref[...] = (nak * lax.rsqrt(_seg_mean_sq(nak, ones64_ref, HEAD_DIM) + EPS) * gnk_ref[...]).astype(BF16)
    nv_ref[...] = u[:, base + 2 * W_SLOT:base + 3 * W_SLOT].astype(BF16)
    base += 3 * W_SLOT
    dfq = u[:, base:base + W_SLOT]
    dfk = u[:, base + W_SLOT:base + 2 * W_SLOT]
    dq_ref[...] = (dfq * lax.rsqrt(_seg_mean_sq(dfq, ones32_ref, DIFF_D) + EPS)
                   * (gdq_ref[...] * (DIFF_D ** -0.5 * LOG2E))).astype(BF16)
    dk_ref[...] = (dfk * lax.rsqrt(_seg_mean_sq(dfk, ones32_ref, DIFF_D) + EPS) * gdk_ref[...]).astype(BF16)
    dv_ref[...] = u[:, base + 2 * W_SLOT:base + 3 * W_SLOT].astype(BF16)


def _prep(h, consts_a, cos, sin, consts_b, seq):
    t, d = h.shape
    tm = _token_tile(t, seq)
    tiles_per_seq = seq // tm
    row = pl.BlockSpec((tm, d), lambda i: (i, 0))
    pos = pl.BlockSpec((tm, LANES), lambda i: (i % tiles_per_seq, 0))
    widths = [H_MLA * LANES, H_MLA * LANES] + [W_SLOT] * 7
    return pl.pallas_call(
        _prep_kernel,
        grid=(t // tm,),
        in_specs=[row] + [_const_spec(c.shape) for c in consts_a] + [pos, pos]
        + [_const_spec(c.shape) for c in consts_b],
        out_specs=[pl.BlockSpec((tm, w), lambda i: (i, 0)) for w in widths],
        out_shape=[jax.ShapeDtypeStruct((t, w), BF16) for w in widths],
        compiler_params=pltpu.CompilerParams(dimension_semantics=("parallel",), vmem_limit_bytes=VMEM_LIMIT),
        name="mix_prep",
    )(h, *consts_a, cos, sin, *consts_b)


def _lane_band(x, lo, width):
    lane = lax.broadcasted_iota(jnp.int32, x.shape, 1)
    return jnp.where((lane >= lo) & (lane < lo + width), x, jnp.zeros_like(x))


def _pipelined_softmax_pv(score_fns, vt_fns, s_bufs, n_keys, chunk, value_rows=HEAD_DIM):
    n_chunk = n_keys // chunk
    rows = lambda c: slice(c * chunk, (c + 1) * chunk)

    def score_pass(u, c, m):
        s = score_fns[u](c)
        s_bufs[u % 2][rows(c), :] = s
        cm = jnp.max(s, axis=0, keepdims=True)
        return cm if m is None else jnp.maximum(m, cm)

    m = None
    for c in range(n_chunk):
        m = score_pass(0, c, m)
    results = []
    for u in range(len(score_fns)):
        m_next = acc = l = None
        for c in range(n_chunk):
            if u + 1 < len(score_fns):
                m_next = score_pass(u + 1, c, m_next)
            p = jnp.exp2(s_bufs[u % 2][rows(c), :] - m)
            vt = vt_fns[u](c)
            ca = _dot(vt, p.astype(BF16))
            acc = ca if acc is None else acc + ca
            if vt.shape[0] == value_rows:
                cl = jnp.sum(p, axis=0, keepdims=True)
                l = cl if l is None else l + cl
        results.append((acc[:value_rows], acc[value_rows:value_rows + 1] if l is None else l))
        m = m_next
    return results


def _store_head_pairs(o_ref, heads_t):
    if len(heads_t) % 2:
        heads_t = heads_t + [jnp.zeros_like(heads_t[0])]
    for p in range(len(heads_t) // 2):
        pair_t = jnp.concatenate(heads_t[2 * p:2 * p + 2], axis=0)
        o_ref[0, :, p * LANES:(p + 1) * LANES] = pair_t.T.astype(o_ref.dtype)


def _mla_attn_kernel(q_ref, k_ref, vt_ref, o_ref, s0_ref, s1_ref):
    n_keys = k_ref.shape[1]
    chunk = min(KEY_CHUNK, n_keys)
    head_cols = lambda h: slice(h * LANES, (h + 1) * LANES)
    key_rows = lambda c: slice(c * chunk, (c + 1) * chunk)
    score_fns = [lambda c, h=h: _dot_nt(k_ref[0, key_rows(c), head_cols(h)], q_ref[0, :, head_cols(h)])
                 for h in range(H_MLA)]
    vt_fns = [lambda c, h=h: vt_ref[0, h * HEAD_DIM:(h + 1) * HEAD_DIM, key_rows(c)] for h in range(H_MLA)]
    res = _pipelined_softmax_pv(score_fns, vt_fns, (s0_ref, s1_ref), n_keys, chunk)
    _store_head_pairs(o_ref, [acc * (1.0 / l) for acc, l in res])


def _mla_attn(q, k, vt, tq):
    b, s, _ = q.shape
    return pl.pallas_call(
        _mla_attn_kernel,
        grid=(b, s // tq),
        in_specs=[pl.BlockSpec((1, tq, H_MLA * LANES), lambda bi, qi: (bi, qi, 0)),
                  pl.BlockSpec((1, s, H_MLA * LANES), lambda bi, qi: (bi, 0, 0)),
                  pl.BlockSpec((1, W_SLOT, s), lambda bi, qi: (bi, 0, 0))],
        out_specs=pl.BlockSpec((1, tq, W_SLOT), lambda bi, qi: (bi, qi, 0)),
        out_shape=jax.ShapeDtypeStruct((b, s, W_SLOT), BF16),
        scratch_shapes=[pltpu.VMEM((s, tq), F32), pltpu.VMEM((s, tq), F32)],
        compiler_params=pltpu.CompilerParams(dimension_semantics=("parallel", "parallel"),
                                             vmem_limit_bytes=VMEM_LIMIT),
        name="mla_attn",
    )(q, k, vt)


def _diff_attn_kernel(lam_ref, q_ref, k_ref, vt_ref, bias_ref, o_ref, s0_ref, s1_ref):
    lam = lam_ref[0]
    n_keys = k_ref.shape[1]
    chunk = min(KEY_CHUNK, n_keys)
    key_rows = lambda c: slice(c * chunk, (c + 1) * chunk)
    score_fns, vt_fns = [], []
    for h in range(H_DIFF):
        slot = slice((h // 2) * LANES, (h // 2 + 1) * LANES)
        for m in range(2):
            qm = _lane_band(q_ref[0, :, slot], (2 * (h % 2) + m) * DIFF_D, DIFF_D)
            score_fns.append(lambda c, h=h, slot=slot, qm=qm:
                             _dot_nt(k_ref[0, key_rows(c), slot], qm) + bias_ref[h, key_rows(c), :])
            vt_fns.append(lambda c, h=h: vt_ref[0, h * VT_ROWS:(h + 1) * VT_ROWS, key_rows(c)])
    res = _pipelined_softmax_pv(score_fns, vt_fns, (s0_ref, s1_ref), n_keys, chunk)
    heads_t = []
    for h in range(H_DIFF):
        (a1, l1), (a2, l2) = res[2 * h], res[2 * h + 1]
        heads_t.append(a1 * (1.0 / l1) - a2 * (lam / l2))
    _store_head_pairs(o_ref, heads_t)


def _diff_attn(lam, q, k, vt, bias_t, tq):
    b, s, _ = q.shape
    return pl.pallas_call(
        _diff_attn_kernel,
        grid=(s // tq, b),
        in_specs=[pl.BlockSpec(memory_space=pltpu.SMEM),
                  pl.BlockSpec((1, tq, W_SLOT), lambda qi, bi: (bi, qi, 0)),
                  pl.BlockSpec((1, s, W_SLOT), lambda qi, bi: (bi, 0, 0)),
                  pl.BlockSpec((1, H_DIFF * VT_ROWS, s), lambda qi, bi: (bi, 0, 0)),
                  pl.BlockSpec((H_DIFF, s, tq), lambda qi, bi: (0, 0, qi), pipeline_mode=pl.Buffered(1))],
        out_specs=pl.BlockSpec((1, tq, W_SLOT), lambda qi, bi: (bi, qi, 0)),
        out_shape=jax.ShapeDtypeStruct((b, s, W_SLOT), BF16),
        scratch_shapes=[pltpu.VMEM((s, tq), F32), pltpu.VMEM((s, tq), F32)],
        compiler_params=pltpu.CompilerParams(dimension_semantics=("parallel", "parallel"),
                                             vmem_limit_bytes=VMEM_LIMIT),
        name="diff_attn",
    )(lam, q, k, vt, bias_t)


def _na_plan(seq):
    rows = seq // GRID_W
    kr = min(NA_KR_MAX, rows)
    assert rows % NA_QROWS == 0 and rows >= NA_WROWS
    patterns, blocks = [], []
    for r0 in range(0, rows, NA_QROWS):
        ws = int(np.clip(r0 - kr // 2, 0, rows - NA_WROWS))
        rs = [int(np.clip(r0 + j - kr // 2, 0, rows - kr)) for j in range(NA_QROWS)]
        assert all(ws <= x and x + kr <= ws + NA_WROWS for x in rs)
        pat = (r0 - ws, tuple(x - ws for x in rs))
        if pat not in patterns:
            patterns.append(pat)
        blocks.append((r0, ws, patterns.index(pat)))
    return kr, patterns, blocks


def _na_bias_table(rpb, seq):
    kr, patterns, _ = _na_plan(seq)
    nq, nk = NA_QROWS * GRID_W, NA_WROWS * GRID_W
    c = np.arange(GRID_W)
    cstart = np.clip(c - NA_KC // 2, 0, GRID_W - NA_KC)
    col_ok = (c[None, :] >= cstart[:, None]) & (c[None, :] < cstart[:, None] + NA_KC)
    col_sel = (c[None, :, None] - c[:, None, None] + NA_KC - 1) == np.arange(2 * NA_KC - 1)
    j, i = np.arange(NA_QROWS), np.arange(NA_WROWS)
    row_ok, row_sel = [], []
    for d0, rs_rel in patterns:
        rs = np.asarray(rs_rel)
        row_ok.append((i[None, :] >= rs[:, None]) & (i[None, :] < rs[:, None] + kr))
        row_sel.append((i[None, :, None] - (d0 + j[:, None, None]) + NA_KR_MAX - 1) == np.arange(2 * NA_KR_MAX - 1))
    row_ok, row_sel = np.stack(row_ok), np.stack(row_sel)
    rows = jnp.einsum('pjia,hab->hpjib', jnp.asarray(row_sel, F32), rpb * LOG2E, precision=lax.Precision.HIGHEST)
    vals = jnp.einsum('hpjib,ckb->hpikjc', rows, jnp.asarray(col_sel, F32), precision=lax.Precision.HIGHEST)
    ok = row_ok.transpose(0, 2, 1)[:, :, None, :, None] & col_ok.T[None, None, :, None, :]
    tab = jnp.where(jnp.asarray(ok)[None], vals, MASK_VALUE).reshape(H_NA, len(patterns), nk, nq)
    tab = jnp.concatenate([tab, jnp.zeros((2 * N_PAIR - H_NA,) + tab.shape[1:], F32)], axis=0)
    tab = tab.reshape(N_PAIR, 2, len(patterns), nk, nq).transpose(0, 2, 3, 1, 4)
    return tab.reshape(N_PAIR, len(patterns), nk, 2 * nq)


def _na_attn_kernel(q_ref, k_ref, vt_ref, bias_ref, o_ref, s0_ref, s1_ref, *, blocks):
    nq, nk = NA_QROWS * GRID_W, NA_WROWS * GRID_W
    score_fns, vt_fns, places = [], [], []
    for r0, ws, pat in blocks:
        q_rows = slice(r0 * GRID_W, r0 * GRID_W + nq)
        k_rows = slice(ws * GRID_W, ws * GRID_W + nk)
        for p in range(N_PAIR):
            slot = slice(p * LANES, (p + 1) * LANES)

            def score(c, q_rows=q_rows, k_rows=k_rows, slot=slot, p=p, pat=pat):
                q = q_ref[0, q_rows, slot]
                q2 = jnp.concatenate([_lane_band(q, 0, HEAD_DIM), _lane_band(q, HEAD_DIM, HEAD_DIM)], axis=0)
                return _dot_nt(k_ref[0, k_rows, slot], q2) + bias_ref[p, pat]

            score_fns.append(score)
            vt_fns.append(lambda c, k_rows=k_rows, p=p: vt_ref[0, p * NA_VT_ROWS:(p + 1) * NA_VT_ROWS, k_rows])
            places.append((q_rows, slot))
    res = _pipelined_softmax_pv(score_fns, vt_fns, (s0_ref, s1_ref), nk, nk, value_rows=LANES)
    for (acc, l), (q_rows, slot) in zip(res, places):
        o = acc * (1.0 / l)
        pair_t = jnp.concatenate([o[:HEAD_DIM, :nq], o[HEAD_DIM:, nq:]], axis=0)
        o_ref[0, q_rows, slot] = pair_t.T.astype(o_ref.dtype)


def _na_attn(q, k, vt, bias_tab):
    b, s, _ = q.shape
    _, _, blocks = _na_plan(s)
    nq, nk = NA_QROWS * GRID_W, NA_WROWS * GRID_W
    tok = pl.BlockSpec((1, s, W_SLOT), lambda bi: (bi, 0, 0))
    return pl.pallas_call(
        functools.partial(_na_attn_kernel, blocks=tuple(blocks)),
        grid=(b,),
        in_specs=[tok, tok, pl.BlockSpec((1, N_PAIR * NA_VT_ROWS, s), lambda bi: (bi, 0, 0)),
                  _const_spec(bias_tab.shape)],
        out_specs=tok,
        out_shape=jax.ShapeDtypeStruct((b, s, W_SLOT), BF16),
        scratch_shapes=[pltpu.VMEM((nk, 2 * nq), F32), pltpu.VMEM((nk, 2 * nq), F32)],
        compiler_params=pltpu.CompilerParams(dimension_semantics=("parallel",), vmem_limit_bytes=VMEM_LIMIT),
        name="na_attn",
    )(q, k, vt, bias_tab)


def _pad_cols(w, groups):
    parts = []
    for start, width, padded in groups:
        parts.append(w[..., start:start + width])
        if padded > width:
            parts.append(jnp.zeros(w.shape[:-1] + (padded - width,), w.dtype))
    return jnp.concatenate(parts, axis=-1)


def _t5_bucket(rel):
    half = T5_BUCKETS // 2
    max_exact = half // 2
    n = jnp.abs(rel)
    nf = jnp.maximum(n, 1).astype(F32)
    large = max_exact + (jnp.log(nf / max_exact) / math.log(T5_MAX_DIST / max_exact)
                         * (half - max_exact)).astype(jnp.int32)
    large = jnp.minimum(large, half - 1)
    return jnp.where(rel > 0, half, 0) + jnp.where(n < max_exact, n, large)


def _t5_bias_kernel(thr_ref, tab_ref, o_ref):
    half = T5_BUCKETS // 2
    head = pl.program_id(0)
    tk, s = o_ref.shape[1:]
    k = pl.program_id(1) * tk + lax.broadcasted_iota(jnp.int32, (tk, s), 0)
    rel = k - lax.broadcasted_iota(jnp.int32, (tk, s), 1)
    n = jnp.abs(rel)
    neg = jnp.full((tk, s), tab_ref[head * T5_BUCKETS], F32)
    pos = jnp.full((tk, s), tab_ref[head * T5_BUCKETS + half], F32)
    for j in range(1, half):
        ge = n >= thr_ref[j]
        neg = jnp.where(ge, tab_ref[head * T5_BUCKETS + j], neg)
        pos = jnp.where(ge, tab_ref[head * T5_BUCKETS + half + j], pos)
    o_ref[0] = jnp.where(rel > 0, pos, neg)


def _t5_bias(t5_table, seq, tq):
    half = T5_BUCKETS // 2
    dist_bucket = _t5_bucket(-jnp.arange(seq, dtype=jnp.int32))
    thr = jnp.sum(dist_bucket[None, :] < jnp.arange(half, dtype=jnp.int32)[:, None], axis=1).astype(jnp.int32)
    tab = t5_table.T * LOG2E
    return pl.pallas_call(
        _t5_bias_kernel,
        grid=(H_DIFF, seq // tq),
        in_specs=[pl.BlockSpec(memory_space=pltpu.SMEM), pl.BlockSpec(memory_space=pltpu.SMEM)],
        out_specs=pl.BlockSpec((1, tq, seq), lambda h, ki: (h, ki, 0)),
        out_shape=jax.ShapeDtypeStruct((H_DIFF, seq, seq), F32),
        compiler_params=pltpu.CompilerParams(dimension_semantics=("parallel", "parallel"),
                                             vmem_limit_bytes=VMEM_LIMIT),
        name="t5_bias",
    )(thr, tab.reshape(-1))


def _swap_rope_lanes(w):
    r = MLA_ROPE // 2
    return jnp.concatenate([w[..., :MLA_NOPE], w[..., MLA_NOPE + r:], w[..., MLA_NOPE:MLA_NOPE + r]], axis=-1)


def _pad_lanes(w, width=LANES):
    return jnp.concatenate([w, jnp.zeros(w.shape[:-1] + (width - w.shape[-1],), w.dtype)], axis=-1)


def _block_ones(width, seg):
    idx = np.arange(width) // seg
    return jnp.asarray(idx[:, None] == idx[None, :], dtype=BF16)


def kernel(x, ffn1_norm, ffn1_w_gate, ffn1_w_up, ffn1_w_down, mix_norm, w_in, mla_q_norm, mla_w_uq, mla_kv_norm, mla_w_ukv, mla_qk_q_norm, mla_qk_k_norm, mla_out_norm, na_qk_q_norm, na_qk_k_norm, na_rpb, na_out_norm, diff_qk_q_norm, diff_qk_k_norm, diff_lambda_q1, diff_lambda_k1, diff_lambda_q2, diff_lambda_k2, diff_subln, t5_table, w_out, ffn2_norm, ffn2_w_gate, ffn2_w_up, ffn2_w_down):
    b, s, d = x.shape
    depth = w_in.shape[0]
    t = b * s
    tq = min(512, s)

    inv_freq = ROPE_BASE ** (-jnp.arange(0, MLA_ROPE, 2, dtype=F32) / MLA_ROPE)
    ang = jnp.arange(s, dtype=F32)[:, None] * inv_freq[None, :]
    cos, sin = jnp.cos(ang), jnp.sin(ang)
    cos_tab = jnp.concatenate([jnp.ones((s, MLA_NOPE), F32), cos, cos, jnp.zeros((s, LANES - MLA_QK), F32)], axis=-1)
    sin_tab = jnp.concatenate([jnp.zeros((s, MLA_NOPE), F32), -sin, sin, jnp.zeros((s, LANES - MLA_QK), F32)], axis=-1)

    t5_bias = _t5_bias(t5_table, s, tq)
    ones64 = _block_ones(W_SLOT, HEAD_DIM)
    ones32 = _block_ones(W_SLOT, DIFF_D)
    seg = H_NA * HEAD_DIM
    in_groups = [(0, Q_LORA + KV_LORA + MLA_ROPE, 2 * Q_LORA)] + [
        (Q_LORA + KV_LORA + MLA_ROPE + i * seg, seg, W_SLOT) for i in range(6)]

    h = x.reshape(t, d)
    for l in range(depth):
        row = lambda v: v[l][None, :]
        h = _ffn(h, row(ffn1_norm), ffn1_w_gate[l].astype(BF16), ffn1_w_up[l].astype(BF16),
                 ffn1_w_down[l].astype(BF16), s)

        win = _pad_cols(w_in[l], in_groups).astype(BF16)
        wuq = mla_w_uq[l].reshape(Q_LORA, H_MLA, MLA_QK)
        wq_all = jnp.concatenate([_pad_lanes(wuq).reshape(Q_LORA, -1),
                                  _pad_lanes(_swap_rope_lanes(wuq)).reshape(Q_LORA, -1)], axis=-1).astype(BF16)
        wukv = mla_w_ukv[l].reshape(KV_LORA, H_MLA, MLA_NOPE + HEAD_DIM)
        eye = jnp.broadcast_to(jnp.eye(MLA_ROPE, dtype=F32)[:, None, :], (MLA_ROPE, H_MLA, MLA_ROPE))
        k_top = jnp.concatenate([wukv[..., :MLA_NOPE], jnp.zeros((KV_LORA, H_MLA, MLA_ROPE), F32)], axis=-1)
        k_bot = jnp.concatenate([jnp.zeros((MLA_ROPE, H_MLA, MLA_NOPE), F32), eye], axis=-1)
        k_rows = jnp.concatenate([k_top, k_bot, jnp.zeros((KV_LORA - MLA_ROPE, H_MLA, MLA_QK), F32)], axis=0)
        v_rows = jnp.concatenate([wukv[..., MLA_NOPE:].reshape(KV_LORA, -1),
                                  jnp.zeros((KV_LORA, H_MLA * HEAD_DIM), F32)], axis=0)
        wk_all = jnp.concatenate([_pad_lanes(k_rows).reshape(2 * KV_LORA, -1),
                                  _pad_lanes(_swap_rope_lanes(k_rows)).reshape(2 * KV_LORA, -1),
                                  v_rows], axis=-1).astype(BF16)
        gq, gk = mla_qk_q_norm[l], mla_qk_k_norm[l]
        tile_pairs = lambda g, n: _pad_lanes(jnp.tile(g, n), W_SLOT)[None, :]
        consts_a = [row(mix_norm), win, row(mla_q_norm), wq_all, row(mla_kv_norm), wk_all,
                    _pad_lanes(gq)[None, :], _pad_lanes(_swap_rope_lanes(gq))[None, :],
                    _pad_lanes(gk)[None, :], _pad_lanes(_swap_rope_lanes(gk))[None, :]]
        consts_b = [ones64, ones32, tile_pairs(na_qk_q_norm[l], H_NA), tile_pairs(na_qk_k_norm[l], H_NA),
                    tile_pairs(diff_qk_q_norm[l], 2 * H_DIFF), tile_pairs(diff_qk_k_norm[l], 2 * H_DIFF)]
        qm, km, vm, nq, nk, nv, dq, dk, dv = _prep(h, consts_a, cos_tab, sin_tab, consts_b, s)
        seq3 = lambda a: a.reshape(b, s, a.shape[-1])

        seq3_t = lambda a: jnp.swapaxes(seq3(a), 1, 2)

        def groups_t_with_ones(v, n_groups, width):
            vg = v[:, :n_groups * width].reshape(b, s, n_groups, width)
            vg = jnp.concatenate([vg, jnp.ones((b, s, n_groups, 16), BF16)], axis=-1)
            return jnp.swapaxes(vg.reshape(b, s, n_groups * (width + 16)), 1, 2)

        o_mla = _mla_attn(seq3(qm), seq3(km), seq3_t(vm), tq)
        o_na = _na_attn(seq3(nq), seq3(nk), groups_t_with_ones(nv, N_PAIR, LANES), _na_bias_table(na_rpb[l], s))
        lam_init = 0.8 - 0.6 * math.exp(-0.3 * l)
        lam = (jnp.exp(jnp.sum(diff_lambda_q1[l] * diff_lambda_k1[l]))
               - jnp.exp(jnp.sum(diff_lambda_q2[l] * diff_lambda_k2[l])) + lam_init).reshape(1)
        o_diff = _diff_attn(lam, seq3(dq), seq3(dk), groups_t_with_ones(dv, H_DIFF, HEAD_DIM), t5_bias, tq)

        wo = w_out[l]
        wo_all = jnp.concatenate([wo[:H_MLA * HEAD_DIM],
                                  wo[H_MLA * HEAD_DIM:H_MLA * HEAD_DIM + seg], jnp.zeros((W_SLOT - seg, d), F32),
                                  wo[H_MLA * HEAD_DIM + seg:], jnp.zeros((W_SLOT - seg, d), F32)], axis=0).astype(BF16)
        h = _outproj_ffn(h, o_mla.reshape(t, -1), o_na.reshape(t, -1), o_diff.reshape(t, -1),
                         row(mla_out_norm), _pad_lanes(na_out_norm[l], W_SLOT)[None, :],
                         tile_pairs(diff_subln[l], H_DIFF), ones64, wo_all,
                         row(ffn2_norm), ffn2_w_gate[l].astype(BF16), ffn2_w_up[l].astype(BF16),
                         ffn2_w_down[l].astype(BF16), s, 1.0 - lam_init)
    return h.reshape(b, s, d)
```

```python
import functools
import math

import numpy as np
import jax
import jax.numpy as jnp
from jax import lax
from jax.experimental import pallas as pl
from jax.experimental.pallas import tpu as pltpu

D_MODEL = 1024
HEAD_DIM = 64
H_MLA, H_NA, H_DIFF = 6, 5, 5
Q_LORA, KV_LORA = 256, 128
MLA_NOPE, MLA_ROPE = 64, 32
MLA_QK = MLA_NOPE + MLA_ROPE
ROPE_BASE = 10000.0
GRID_W = 64
NA_KR_MAX, NA_KC = 8, 16
DIFF_D = 32
T5_BUCKETS, T5_MAX_DIST = 32, 128
D_FF = 2816
EPS = 1e-6

LANES = 128
N_PAIR = 3
W_SLOT = N_PAIR * LANES
NP_IN = 512 + 6 * W_SLOT
NA_QROWS = 4
KEY_CHUNK = 1024
VT_ROWS = HEAD_DIM + 16
NA_VT_ROWS = LANES + 16
NA_WROWS = 12
MASK_VALUE = -1e30
LOG2E = math.log2(math.e)
BOUNDED_LOGIT_LIMIT = 60.0
VMEM_LIMIT = 56 * 1024 * 1024

BF16 = jnp.bfloat16
F32 = jnp.float32


def _dot(a, b):
    return jnp.dot(a, b, preferred_element_type=F32)


def _dot_nt(a, b):
    return lax.dot_general(a, b, (((1,), (1,)), ((), ())), preferred_element_type=F32)


def _const_spec(shape):
    nd = len(shape)
    return pl.BlockSpec(shape, lambda *_: (0,) * nd, pipeline_mode=pl.Buffered(1))


def _swiglu_half(h, g_ref, wg_ref, wu_ref, wd_ref):
    ms = jnp.mean(h * h, axis=-1, keepdims=True)
    xn = (h * lax.rsqrt(ms + EPS) * g_ref[...]).astype(BF16)
    gate = _dot(xn, wg_ref[...])
    up = _dot(xn, wu_ref[...])
    act = (gate * (1.0 / (1.0 + jnp.exp(-gate))) * up).astype(BF16)
    return h + 0.5 * _dot(act, wd_ref[...])


def _ffn_kernel(h_ref, g_ref, wg_ref, wu_ref, wd_ref, o_ref):
    o_ref[...] = _swiglu_half(h_ref[...], g_ref, wg_ref, wu_ref, wd_ref)


def _seg_mean_sq(x, ones_ref, seg):
    return _dot((x * x).astype(BF16), ones_ref[...]) * (1.0 / seg)


def _outproj_ffn_kernel(h_ref, om_ref, on_ref, od_ref, gm_ref, gn_ref, gd_ref, ones64_ref, wo_ref,
                        g_ref, wg_ref, wu_ref, wd_ref, o_ref, *, diff_out_scale):
    om = om_ref[...].astype(F32)
    on = on_ref[...].astype(F32)
    od = od_ref[...].astype(F32)
    om = om * lax.rsqrt(jnp.sum(om * om, axis=-1, keepdims=True) * (1.0 / (H_MLA * HEAD_DIM)) + EPS) * gm_ref[...]
    on = on * lax.rsqrt(jnp.sum(on * on, axis=-1, keepdims=True) * (1.0 / (H_NA * HEAD_DIM)) + EPS) * gn_ref[...]
    od = od * lax.rsqrt(_seg_mean_sq(od, ones64_ref, HEAD_DIM) + EPS) * (gd_ref[...] * diff_out_scale)
    cat = jnp.concatenate([om, on, od], axis=-1).astype(BF16)
    h = h_ref[...] + _dot(cat, wo_ref[...])
    o_ref[...] = _swiglu_half(h, g_ref, wg_ref, wu_ref, wd_ref)


def _token_tile(n_tokens, seq):
    tm = min(512, seq)
    assert seq % tm == 0 and n_tokens % tm == 0
    return tm


def _ffn(h, g, wg, wu, wd, seq):
    t, d = h.shape
    tm = _token_tile(t, seq)
    row = pl.BlockSpec((tm, d), lambda i: (i, 0))
    return pl.pallas_call(
        _ffn_kernel,
        grid=(t // tm,),
        in_specs=[row, _const_spec(g.shape), _const_spec(wg.shape), _const_spec(wu.shape), _const_spec(wd.shape)],
        out_specs=row,
        out_shape=jax.ShapeDtypeStruct((t, d), F32),
        compiler_params=pltpu.CompilerParams(dimension_semantics=("parallel",), vmem_limit_bytes=VMEM_LIMIT),
        name="ffn",
    )(h, g, wg, wu, wd)


def _outproj_ffn(h, om, on, od, gm, gn, gd, ones64, wo, g, wg, wu, wd, seq, diff_out_scale):
    t, d = h.shape
    tm = _token_tile(t, seq)
    row = pl.BlockSpec((tm, d), lambda i: (i, 0))
    slot = pl.BlockSpec((tm, W_SLOT), lambda i: (i, 0))
    consts = [gm, gn, gd, ones64, wo, g, wg, wu, wd]
    return pl.pallas_call(
        functools.partial(_outproj_ffn_kernel, diff_out_scale=diff_out_scale),
        grid=(t // tm,),
        in_specs=[row, slot, slot, slot] + [_const_spec(c.shape) for c in consts],
        out_specs=row,
        out_shape=jax.ShapeDtypeStruct((t, d), F32),
        compiler_params=pltpu.CompilerParams(dimension_semantics=("parallel",), vmem_limit_bytes=VMEM_LIMIT),
        name="outproj_ffn",
    )(h, om, on, od, *consts)


def _prep_kernel(h_ref, gmix_ref, win_ref, gq_ref, wq_ref, gkv_ref, wk_ref,
                 gqc_ref, gqs_ref, gkc_ref, gks_ref, cos_ref, sin_ref,
                 ones64_ref, ones32_ref, gnq_ref, gnk_ref, gdq_ref, gdk_ref,
                 qm_ref, km_ref, vm_ref, nq_ref, nk_ref, nv_ref, dq_ref, dk_ref, dv_ref):
    h = h_ref[...]
    ms = jnp.mean(h * h, axis=-1, keepdims=True)
    xn = (h * lax.rsqrt(ms + EPS) * gmix_ref[...]).astype(BF16)
    u = _dot(xn, win_ref[...])

    cq = u[:, 0:Q_LORA]
    cqn = (cq * lax.rsqrt(jnp.mean(cq * cq, axis=-1, keepdims=True) + EPS) * gq_ref[...]).astype(BF16)
    q2 = _dot(cqn, wq_ref[...])
    ckv = u[:, Q_LORA:Q_LORA + KV_LORA]
    ckvn = ckv * lax.rsqrt(jnp.mean(ckv * ckv, axis=-1, keepdims=True) + EPS) * gkv_ref[...]
    kr = u[:, Q_LORA + KV_LORA:Q_LORA + 2 * KV_LORA]
    k2 = _dot(jnp.concatenate([ckvn, kr], axis=-1).astype(BF16), wk_ref[...])

    cos = cos_ref[...]
    sin = sin_ref[...]
    gqc = gqc_ref[...] * cos
    gqs = gqs_ref[...] * sin
    gkc = gkc_ref[...] * cos
    gks = gks_ref[...] * sin
    nsw = H_MLA * LANES
    for hd in range(H_MLA):
        lo, hi = hd * LANES, (hd + 1) * LANES
        qr, qs = q2[:, lo:hi], q2[:, nsw + lo:nsw + hi]
        inv = lax.rsqrt(jnp.sum(qr * qr, axis=-1, keepdims=True) * (1.0 / MLA_QK) + EPS) * (MLA_QK ** -0.5 * LOG2E)
        qm_ref[:, lo:hi] = ((qr * gqc + qs * gqs) * inv).astype(BF16)
        kr_, ks_ = k2[:, lo:hi], k2[:, nsw + lo:nsw + hi]
        inv = lax.rsqrt(jnp.sum(kr_ * kr_, axis=-1, keepdims=True) * (1.0 / MLA_QK) + EPS)
        km_ref[:, lo:hi] = ((kr_ * gkc + ks_ * gks) * inv).astype(BF16)
    vm_ref[...] = k2[:, 2 * nsw:].astype(BF16)

    base = 2 * Q_LORA
    naq = u[:, base:base + W_SLOT]
    nak = u[:, base + W_SLOT:base + 2 * W_SLOT]
    nq_ref[...] = (naq * lax.rsqrt(_seg_mean_sq(naq, ones64_ref, HEAD_DIM) + EPS)
                   * (gnq_ref[...] * (HEAD_DIM ** -0.5 * LOG2E))).astype(BF16)
    nk_ref[...] = (nak * lax.rsqrt(_seg_mean_sq(nak, ones64_ref, HEAD_DIM) + EPS) * gnk_ref[...]).astype(BF16)
    nv_ref[...] = u[:, base + 2 * W_SLOT:base + 3 * W_SLOT].astype(BF16)
    base += 3 * W_SLOT
    dfq = u[:, base:base + W_SLOT]
    dfk = u[:, base + W_SLOT:base + 2 * W_SLOT]
    dq_ref[...] = (dfq * lax.rsqrt(_seg_mean_sq(dfq, ones32_ref, DIFF_D) + EPS)
                   * (gdq_ref[...] * (DIFF_D ** -0.5 * LOG2E))).astype(BF16)
    dk_ref[...] = (dfk * lax.rsqrt(_seg_mean_sq(dfk, ones32_ref, DIFF_D) + EPS) * gdk_ref[...]).astype(BF16)
    dv_ref[...] = u[:, base + 2 * W_SLOT:base + 3 * W_SLOT].astype(BF16)


def _prep(h, consts_a, cos, sin, consts_b, seq):
    t, d = h.shape
    tm = _token_tile(t, seq)
    tiles_per_seq = seq // tm
    row = pl.BlockSpec((tm, d), lambda i: (i, 0))
    pos = pl.BlockSpec((tm, LANES), lambda i: (i % tiles_per_seq, 0))
    widths = [H_MLA * LANES, H_MLA * LANES] + [W_SLOT] * 7
    return pl.pallas_call(
        _prep_kernel,
        grid=(t // tm,),
        in_specs=[row] + [_const_spec(c.shape) for c in consts_a] + [pos, pos]
        + [_const_spec(c.shape) for c in consts_b],
        out_specs=[pl.BlockSpec((tm, w), lambda i: (i, 0)) for w in widths],
        out_shape=[jax.ShapeDtypeStruct((t, w), BF16) for w in widths],
        compiler_params=pltpu.CompilerParams(dimension_semantics=("parallel",), vmem_limit_bytes=VMEM_LIMIT),
        name="mix_prep",
    )(h, *consts_a, cos, sin, *consts_b)


def _lane_band(x, lo, width):
    lane = lax.broadcasted_iota(jnp.int32, x.shape, 1)
    return jnp.where((lane >= lo) & (lane < lo + width), x, jnp.zeros_like(x))


def _pipelined_softmax_pv(score_fns, vt_fns, s_bufs, n_keys, chunk, value_rows=HEAD_DIM):
    n_chunk = n_keys // chunk
    rows = lambda c: slice(c * chunk, (c + 1) * chunk)

    def accumulate(u, c, p, acc, l):
        vt = vt_fns[u](c)
        ca = _dot(vt, p.astype(BF16))
        acc = ca if acc is None else acc + ca
        if vt.shape[0] == value_rows:
            cl = jnp.sum(p, axis=0, keepdims=True)
            l = cl if l is None else l + cl
        return acc, l

    finish = lambda acc, l: (acc[:value_rows], acc[value_rows:value_rows + 1] if l is None else l)

    if s_bufs is None:
        results = []
        for u in range(len(score_fns)):
            acc = l = None
            for c in range(n_chunk):
                acc, l = accumulate(u, c, jnp.exp2(score_fns[u](c)), acc, l)
            results.append(finish(acc, l))
        return results

    def score_pass(u, c, m):
        s = score_fns[u](c)
        s_bufs[u % 2][rows(c), :] = s
        cm = jnp.max(s, axis=0, keepdims=True)
        return cm if m is None else jnp.maximum(m, cm)

    m = None
    for c in range(n_chunk):
        m = score_pass(0, c, m)
    results = []
    for u in range(len(score_fns)):
        m_next = acc = l = None
        for c in range(n_chunk):
            if u + 1 < len(score_fns):
                m_next = score_pass(u + 1, c, m_next)
            acc, l = accumulate(u, c, jnp.exp2(s_bufs[u % 2][rows(c), :] - m), acc, l)
        results.append(finish(acc, l))
        m = m_next
    return results


def _store_head_pairs(o_ref, heads_t):
    if len(heads_t) % 2:
        heads_t = heads_t + [jnp.zeros_like(heads_t[0])]
    for p in range(len(heads_t) // 2):
        pair_t = jnp.concatenate(heads_t[2 * p:2 * p + 2], axis=0)
        o_ref[0, :, p * LANES:(p + 1) * LANES] = pair_t.T.astype(o_ref.dtype)


def _mla_attn_kernel(q_ref, k_ref, vt_ref, o_ref, *s_bufs):
    n_keys = k_ref.shape[1]
    chunk = min(KEY_CHUNK, n_keys)
    head_cols = lambda h: slice(h * LANES, (h + 1) * LANES)
    key_rows = lambda c: slice(c * chunk, (c + 1) * chunk)
    score_fns = [lambda c, h=h: _dot_nt(k_ref[0, key_rows(c), head_cols(h)], q_ref[0, :, head_cols(h)])
                 for h in range(H_MLA)]
    vt_fns = [lambda c, h=h: vt_ref[0, h * HEAD_DIM:(h + 1) * HEAD_DIM, key_rows(c)] for h in range(H_MLA)]
    res = _pipelined_softmax_pv(score_fns, vt_fns, s_bufs or None, n_keys, chunk)
    _store_head_pairs(o_ref, [acc * (1.0 / l) for acc, l in res])


def _score_scratch(bounded, rows, cols):
    return [] if bounded else [pltpu.VMEM((rows, cols), F32), pltpu.VMEM((rows, cols), F32)]


def _mla_attn(q, k, vt, tq, bounded):
    b, s, _ = q.shape
    return pl.pallas_call(
        _mla_attn_kernel,
        grid=(b, s // tq),
        in_specs=[pl.BlockSpec((1, tq, H_MLA * LANES), lambda bi, qi: (bi, qi, 0)),
                  pl.BlockSpec((1, s, H_MLA * LANES), lambda bi, qi: (bi, 0, 0)),
                  pl.BlockSpec((1, W_SLOT, s), lambda bi, qi: (bi, 0, 0))],
        out_specs=pl.BlockSpec((1, tq, W_SLOT), lambda bi, qi: (bi, qi, 0)),
        out_shape=jax.ShapeDtypeStruct((b, s, W_SLOT), BF16),
        scratch_shapes=_score_scratch(bounded, s, tq),
        compiler_params=pltpu.CompilerParams(dimension_semantics=("parallel", "parallel"),
                                             vmem_limit_bytes=VMEM_LIMIT),
        name="mla_attn",
    )(q, k, vt)


def _diff_attn_kernel(lam_ref, q_ref, k_ref, vt_ref, bias_ref, o_ref, *s_bufs, vt_rows):
    lam = lam_ref[0]
    n_keys = k_ref.shape[1]
    chunk = min(KEY_CHUNK, n_keys)
    key_rows = lambda c: slice(c * chunk, (c + 1) * chunk)
    score_fns, vt_fns = [], []
    for h in range(H_DIFF):
        slot = slice((h // 2) * LANES, (h // 2 + 1) * LANES)
        for m in range(2):
            qm = _lane_band(q_ref[0, :, slot], (2 * (h % 2) + m) * DIFF_D, DIFF_D)
            score_fns.append(lambda c, h=h, slot=slot, qm=qm:
                             _dot_nt(k_ref[0, key_rows(c), slot], qm) + bias_ref[h, key_rows(c), :])
            vt_fns.append(lambda c, h=h: vt_ref[0, h * vt_rows:(h + 1) * vt_rows, key_rows(c)])
    res = _pipelined_softmax_pv(score_fns, vt_fns, s_bufs or None, n_keys, chunk)
    heads_t = []
    for h in range(H_DIFF):
        (a1, l1), (a2, l2) = res[2 * h], res[2 * h + 1]
        heads_t.append(a1 * (1.0 / l1) - a2 * (lam / l2))
    _store_head_pairs(o_ref, heads_t)


def _diff_attn(lam, q, k, vt, bias_t, tq, bounded):
    b, s, _ = q.shape
    return pl.pallas_call(
        functools.partial(_diff_attn_kernel, vt_rows=HEAD_DIM if bounded else VT_ROWS),
        grid=(s // tq, b),
        in_specs=[pl.BlockSpec(memory_space=pltpu.SMEM),
                  pl.BlockSpec((1, tq, W_SLOT), lambda qi, bi: (bi, qi, 0)),
                  pl.BlockSpec((1, s, W_SLOT), lambda qi, bi: (bi, 0, 0)),
                  pl.BlockSpec((1, vt.shape[1], s), lambda qi, bi: (bi, 0, 0)),
                  pl.BlockSpec((H_DIFF, s, tq), lambda qi, bi: (0, 0, qi), pipeline_mode=pl.Buffered(1))],
        out_specs=pl.BlockSpec((1, tq, W_SLOT), lambda qi, bi: (bi, qi, 0)),
        out_shape=jax.ShapeDtypeStruct((b, s, W_SLOT), BF16),
        scratch_shapes=_score_scratch(bounded, s, tq),
        compiler_params=pltpu.CompilerParams(dimension_semantics=("parallel", "parallel"),
                                             vmem_limit_bytes=VMEM_LIMIT),
        name="diff_attn",
    )(lam, q, k, vt, bias_t)


def _na_plan(seq):
    rows = seq // GRID_W
    kr = min(NA_KR_MAX, rows)
    assert rows % NA_QROWS == 0 and rows >= NA_WROWS
    patterns, blocks = [], []
    for r0 in range(0, rows, NA_QROWS):
        ws = int(np.clip(r0 - kr // 2, 0, rows - NA_WROWS))
        rs = [int(np.clip(r0 + j - kr // 2, 0, rows - kr)) for j in range(NA_QROWS)]
        assert all(ws <= x and x + kr <= ws + NA_WROWS for x in rs)
        pat = (r0 - ws, tuple(x - ws for x in rs))
        if pat not in patterns:
            patterns.append(pat)
        blocks.append((r0, ws, patterns.index(pat)))
    return kr, patterns, blocks


def _na_bias_table(rpb, seq):
    kr, patterns, _ = _na_plan(seq)
    nq, nk = NA_QROWS * GRID_W, NA_WROWS * GRID_W
    c = np.arange(GRID_W)
    cstart = np.clip(c - NA_KC // 2, 0, GRID_W - NA_KC)
    col_ok = (c[None, :] >= cstart[:, None]) & (c[None, :] < cstart[:, None] + NA_KC)
    col_sel = (c[None, :, None] - c[:, None, None] + NA_KC - 1) == np.arange(2 * NA_KC - 1)
    j, i = np.arange(NA_QROWS), np.arange(NA_WROWS)
    row_ok, row_sel = [], []
    for d0, rs_rel in patterns:
        rs = np.asarray(rs_rel)
        row_ok.append((i[None, :] >= rs[:, None]) & (i[None, :] < rs[:, None] + kr))
        row_sel.append((i[None, :, None] - (d0 + j[:, None, None]) + NA_KR_MAX - 1) == np.arange(2 * NA_KR_MAX - 1))
    row_ok, row_sel = np.stack(row_ok), np.stack(row_sel)
    rows = jnp.einsum('pjia,hab->hpjib', jnp.asarray(row_sel, F32), rpb * LOG2E, precision=lax.Precision.HIGHEST)
    vals = jnp.einsum('hpjib,ckb->hpikjc', rows, jnp.asarray(col_sel, F32), precision=lax.Precision.HIGHEST)
    ok = row_ok.transpose(0, 2, 1)[:, :, None, :, None] & col_ok.T[None, None, :, None, :]
    tab = jnp.where(jnp.asarray(ok)[None], vals, MASK_VALUE).reshape(H_NA, len(patterns), nk, nq)
    tab = jnp.concatenate([tab, jnp.zeros((2 * N_PAIR - H_NA,) + tab.shape[1:], F32)], axis=0)
    return tab.reshape(N_PAIR, 2, len(patterns), nk, nq)


def _na_attn_kernel(q_ref, k_ref, vt_ref, bias_ref, o_ref, *s_bufs, blocks, vt_rows):
    nq, nk = NA_QROWS * GRID_W, NA_WROWS * GRID_W
    score_fns, vt_fns, places = [], [], []
    for r0, ws, pat in blocks:
        q_rows = slice(r0 * GRID_W, r0 * GRID_W + nq)
        k_rows = slice(ws * GRID_W, ws * GRID_W + nk)
        for p in range(N_PAIR):
            slot = slice(p * LANES, (p + 1) * LANES)

            def score(c, q_rows=q_rows, k_rows=k_rows, slot=slot, p=p, pat=pat):
                q = q_ref[0, q_rows, slot]
                q2 = jnp.concatenate([_lane_band(q, 0, HEAD_DIM), _lane_band(q, HEAD_DIM, HEAD_DIM)], axis=0)
                bias = jnp.concatenate([bias_ref[p, 0, pat], bias_ref[p, 1, pat]], axis=1)
                return _dot_nt(k_ref[0, k_rows, slot], q2) + bias

            score_fns.append(score)
            vt_fns.append(lambda c, k_rows=k_rows, p=p: vt_ref[0, p * vt_rows:(p + 1) * vt_rows, k_rows])
            places.append((q_rows, slot))
    res = _pipelined_softmax_pv(score_fns, vt_fns, s_bufs or None, nk, nk, value_rows=LANES)
    for (acc, l), (q_rows, slot) in zip(res, places):
        o = acc * (1.0 / l)
        pair_t = jnp.concatenate([o[:HEAD_DIM, :nq], o[HEAD_DIM:, nq:]], axis=0)
        o_ref[0, q_rows, slot] = pair_t.T.astype(o_ref.dtype)


def _na_attn(q, k, vt, bias_tab, bounded):
    b, s, _ = q.shape
    _, _, blocks = _na_plan(s)
    nq, nk = NA_QROWS * GRID_W, NA_WROWS * GRID_W
    tok = pl.BlockSpec((1, s, W_SLOT), lambda bi: (bi, 0, 0))
    return pl.pallas_call(
        functools.partial(_na_attn_kernel, blocks=tuple(blocks), vt_rows=LANES if bounded else NA_VT_ROWS),
        grid=(b,),
        in_specs=[tok, tok, pl.BlockSpec((1, vt.shape[1], s), lambda bi: (bi, 0, 0)),
                  _const_spec(bias_tab.shape)],
        out_specs=tok,
        out_shape=jax.ShapeDtypeStruct((b, s, W_SLOT), BF16),
        scratch_shapes=_score_scratch(bounded, nk, 2 * nq),
        compiler_params=pltpu.CompilerParams(dimension_semantics=("parallel",), vmem_limit_bytes=VMEM_LIMIT),
        name="na_attn",
    )(q, k, vt, bias_tab)


def _pad_cols(w, groups):
    parts = []
    for start, width, padded in groups:
        parts.append(w[..., start:start + width])
        if padded > width:
            parts.append(jnp.zeros(w.shape[:-1] + (padded - width,), w.dtype))
    return jnp.concatenate(parts, axis=-1)


def _t5_bucket(rel):
    half = T5_BUCKETS // 2
    max_exact = half // 2
    n = jnp.abs(rel)
    nf = jnp.maximum(n, 1).astype(F32)
    large = max_exact + (jnp.log(nf / max_exact) / math.log(T5_MAX_DIST / max_exact)
                         * (half - max_exact)).astype(jnp.int32)
    large = jnp.minimum(large, half - 1)
    return jnp.where(rel > 0, half, 0) + jnp.where(n < max_exact, n, large)


def _t5_bias_kernel(thr_ref, tab_ref, o_ref):
    half = T5_BUCKETS // 2
    head = pl.program_id(0)
    tk, s = o_ref.shape[1:]
    k = pl.program_id(1) * tk + lax.broadcasted_iota(jnp.int32, (tk, s), 0)
    rel = k - lax.broadcasted_iota(jnp.int32, (tk, s), 1)
    n = jnp.abs(rel)
    neg = jnp.full((tk, s), tab_ref[head * T5_BUCKETS], F32)
    pos = jnp.full((tk, s), tab_ref[head * T5_BUCKETS + half], F32)
    for j in range(1, half):
        ge = n >= thr_ref[j]
        neg = jnp.where(ge, tab_ref[head * T5_BUCKETS + j], neg)
        pos = jnp.where(ge, tab_ref[head * T5_BUCKETS + half + j], pos)
    o_ref[0] = jnp.where(rel > 0, pos, neg)


def _t5_bias(t5_table, seq, tq):
    half = T5_BUCKETS // 2
    dist_bucket = _t5_bucket(-jnp.arange(seq, dtype=jnp.int32))
    thr = jnp.sum(dist_bucket[None, :] < jnp.arange(half, dtype=jnp.int32)[:, None], axis=1).astype(jnp.int32)
    tab = t5_table.T * LOG2E
    return pl.pallas_call(
        _t5_bias_kernel,
        grid=(H_DIFF, seq // tq),
        in_specs=[pl.BlockSpec(memory_space=pltpu.SMEM), pl.BlockSpec(memory_space=pltpu.SMEM)],
        out_specs=pl.BlockSpec((1, tq, seq), lambda h, ki: (h, ki, 0)),
        out_shape=jax.ShapeDtypeStruct((H_DIFF, seq, seq), F32),
        compiler_params=pltpu.CompilerParams(dimension_semantics=("parallel", "parallel"),
                                             vmem_limit_bytes=VMEM_LIMIT),
        name="t5_bias",
    )(thr, tab.reshape(-1))


def _swap_rope_lanes(w):
    r = MLA_ROPE // 2
    return jnp.concatenate([w[..., :MLA_NOPE], w[..., MLA_NOPE + r:], w[..., MLA_NOPE:MLA_NOPE + r]], axis=-1)


def _pad_lanes(w, width=LANES):
    return jnp.concatenate([w, jnp.zeros(w.shape[:-1] + (width - w.shape[-1],), w.dtype)], axis=-1)


def _block_ones(width, seg):
    idx = np.arange(width) // seg
    return jnp.asarray(idx[:, None] == idx[None, :], dtype=BF16)


def kernel(x, ffn1_norm, ffn1_w_gate, ffn1_w_up, ffn1_w_down, mix_norm, w_in, mla_q_norm, mla_w_uq, mla_kv_norm, mla_w_ukv, mla_qk_q_norm, mla_qk_k_norm, mla_out_norm, na_qk_q_norm, na_qk_k_norm, na_rpb, na_out_norm, diff_qk_q_norm, diff_qk_k_norm, diff_lambda_q1, diff_lambda_k1, diff_lambda_q2, diff_lambda_k2, diff_subln, t5_table, w_out, ffn2_norm, ffn2_w_gate, ffn2_w_up, ffn2_w_down):
    b, s, d = x.shape
    depth = w_in.shape[0]
    t = b * s
    tq = min(512, s)

    inv_freq = ROPE_BASE ** (-jnp.arange(0, MLA_ROPE, 2, dtype=F32) / MLA_ROPE)
    ang = jnp.arange(s, dtype=F32)[:, None] * inv_freq[None, :]
    cos, sin = jnp.cos(ang), jnp.sin(ang)
    cos_tab = jnp.concatenate([jnp.ones((s, MLA_NOPE), F32), cos, cos, jnp.zeros((s, LANES - MLA_QK), F32)], axis=-1)
    sin_tab = jnp.concatenate([jnp.zeros((s, MLA_NOPE), F32), -sin, sin, jnp.zeros((s, LANES - MLA_QK), F32)], axis=-1)

    t5_bias = _t5_bias(t5_table, s, tq)
    ones64 = _block_ones(W_SLOT, HEAD_DIM)
    ones32 = _block_ones(W_SLOT, DIFF_D)
    seg = H_NA * HEAD_DIM
    in_groups = [(0, Q_LORA + KV_LORA + MLA_ROPE, 2 * Q_LORA)] + [
        (Q_LORA + KV_LORA + MLA_ROPE + i * seg, seg, W_SLOT) for i in range(6)]

    h = x.reshape(t, d)
    for l in range(depth):
        row = lambda v: v[l][None, :]
        h = _ffn(h, row(ffn1_norm), ffn1_w_gate[l].astype(BF16), ffn1_w_up[l].astype(BF16),
                 ffn1_w_down[l].astype(BF16), s)

        win = _pad_cols(w_in[l], in_groups).astype(BF16)
        wuq = mla_w_uq[l].reshape(Q_LORA, H_MLA, MLA_QK)
        wq_all = jnp.concatenate([_pad_lanes(wuq).reshape(Q_LORA, -1),
                                  _pad_lanes(_swap_rope_lanes(wuq)).reshape(Q_LORA, -1)], axis=-1).astype(BF16)
        wukv = mla_w_ukv[l].reshape(KV_LORA, H_MLA, MLA_NOPE + HEAD_DIM)
        eye = jnp.broadcast_to(jnp.eye(MLA_ROPE, dtype=F32)[:, None, :], (MLA_ROPE, H_MLA, MLA_ROPE))
        k_top = jnp.concatenate([wukv[..., :MLA_NOPE], jnp.zeros((KV_LORA, H_MLA, MLA_ROPE), F32)], axis=-1)
        k_bot = jnp.concatenate([jnp.zeros((MLA_ROPE, H_MLA, MLA_NOPE), F32), eye], axis=-1)
        k_rows = jnp.concatenate([k_top, k_bot, jnp.zeros((KV_LORA - MLA_ROPE, H_MLA, MLA_QK), F32)], axis=0)
        v_rows = jnp.concatenate([wukv[..., MLA_NOPE:].reshape(KV_LORA, -1),
                                  jnp.zeros((KV_LORA, H_MLA * HEAD_DIM), F32)], axis=0)
        wk_all = jnp.concatenate([_pad_lanes(k_rows).reshape(2 * KV_LORA, -1),
                                  _pad_lanes(_swap_rope_lanes(k_rows)).reshape(2 * KV_LORA, -1),
                                  v_rows], axis=-1).astype(BF16)
        gq, gk = mla_qk_q_norm[l], mla_qk_k_norm[l]
        tile_pairs = lambda g, n: _pad_lanes(jnp.tile(g, n), W_SLOT)[None, :]
        consts_a = [row(mix_norm), win, row(mla_q_norm), wq_all, row(mla_kv_norm), wk_all,
                    _pad_lanes(gq)[None, :], _pad_lanes(_swap_rope_lanes(gq))[None, :],
                    _pad_lanes(gk)[None, :], _pad_lanes(_swap_rope_lanes(gk))[None, :]]
        consts_b = [ones64, ones32, tile_pairs(na_qk_q_norm[l], H_NA), tile_pairs(na_qk_k_norm[l], H_NA),
                    tile_pairs(diff_qk_q_norm[l], 2 * H_DIFF), tile_pairs(diff_qk_k_norm[l], 2 * H_DIFF)]
        qm, km, vm, nq, nk, nv, dq, dk, dv = _prep(h, consts_a, cos_tab, sin_tab, consts_b, s)
        seq3 = lambda a: a.reshape(b, s, a.shape[-1])

        seq3_t = lambda a: jnp.swapaxes(seq3(a), 1, 2)

        def groups_t_with_ones(v, n_groups, width):
            vg = v[:, :n_groups * width].reshape(b, s, n_groups, width)
            vg = jnp.concatenate([vg, jnp.ones((b, s, n_groups, 16), BF16)], axis=-1)
            return jnp.swapaxes(vg.reshape(b, s, n_groups * (width + 16)), 1, 2)

        lam_init = 0.8 - 0.6 * math.exp(-0.3 * l)
        lam = (jnp.exp(jnp.sum(diff_lambda_q1[l] * diff_lambda_k1[l]))
               - jnp.exp(jnp.sum(diff_lambda_q2[l] * diff_lambda_k2[l])) + lam_init).reshape(1)
        na_bias = _na_bias_table(na_rpb[l], s)

        def attention(bounded):
            if bounded:
                vt_na, vt_diff = seq3_t(nv), seq3_t(dv)
            else:
                vt_na = groups_t_with_ones(nv, N_PAIR, LANES)
                vt_diff = groups_t_with_ones(dv, H_DIFF, HEAD_DIM)
            return (_mla_attn(seq3(qm), seq3(km), seq3_t(vm), tq, bounded),
                    _na_attn(seq3(nq), seq3(nk), vt_na, na_bias, bounded),
                    _diff_attn(lam, seq3(dq), seq3(dk), vt_diff, t5_bias, tq, bounded))

        amax = lambda v: jnp.max(jnp.abs(v))
        logit_bound = LOG2E * jnp.maximum(jnp.maximum(
            math.sqrt(MLA_QK) * amax(mla_qk_q_norm[l]) * amax(mla_qk_k_norm[l]),
            math.sqrt(HEAD_DIM) * amax(na_qk_q_norm[l]) * amax(na_qk_k_norm[l]) + amax(na_rpb[l])),
            math.sqrt(DIFF_D) * amax(diff_qk_q_norm[l]) * amax(diff_qk_k_norm[l]) + amax(t5_table))
        o_mla, o_na, o_diff = lax.cond(logit_bound < BOUNDED_LOGIT_LIMIT,
                                       lambda: attention(True), lambda: attention(False))

        wo = w_out[l]
        wo_all = jnp.concatenate([wo[:H_MLA * HEAD_DIM],
                                  wo[H_MLA * HEAD_DIM:H_MLA * HEAD_DIM + seg], jnp.zeros((W_SLOT - seg, d), F32),
                                  wo[H_MLA * HEAD_DIM + seg:], jnp.zeros((W_SLOT - seg, d), F32)], axis=0).astype(BF16)
        h = _outproj_ffn(h, o_mla.reshape(t, -1), o_na.reshape(t, -1), o_diff.reshape(t, -1),
                         row(mla_out_norm), _pad_lanes(na_out_norm[l], W_SLOT)[None, :],
                         tile_pairs(diff_subln[l], H_DIFF), ones64, wo_all,
                         row(ffn2_norm), ffn2_w_gate[l].astype(BF16), ffn2_w_up[l].astype(BF16),
                         ffn2_w_down[l].astype(BF16), s, 1.0 - lam_init)
    return h.reshape(b, s, d)
```

```python
import functools
import math

import numpy as np
import jax
import jax.numpy as jnp
from jax import lax
from jax.experimental import pallas as pl
from jax.experimental.pallas import tpu as pltpu

D_MODEL = 1024
HEAD_DIM = 64
H_MLA, H_NA, H_DIFF = 6, 5, 5
Q_LORA, KV_LORA = 256, 128
MLA_NOPE, MLA_ROPE = 64, 32
MLA_QK = MLA_NOPE + MLA_ROPE
ROPE_BASE = 10000.0
GRID_W = 64
NA_KR_MAX, NA_KC = 8, 16
DIFF_D = 32
T5_BUCKETS, T5_MAX_DIST = 32, 128
D_FF = 2816
EPS = 1e-6

LANES = 128
N_PAIR = 3
W_SLOT = N_PAIR * LANES
NP_IN = 512 + 6 * W_SLOT
NA_QROWS = 4
KEY_CHUNK = 1024
VT_ROWS = HEAD_DIM + 16
NA_VT_ROWS = LANES + 16
NA_WROWS = 12
MASK_VALUE = -1e30
LOG2E = math.log2(math.e)
BOUNDED_LOGIT_LIMIT = 60.0
VMEM_LIMIT = 56 * 1024 * 1024

BF16 = jnp.bfloat16
F32 = jnp.float32


def _dot(a, b):
    return jnp.dot(a, b, preferred_element_type=F32)


def _dot_nt(a, b):
    return lax.dot_general(a, b, (((1,), (1,)), ((), ())), preferred_element_type=F32)


def _const_spec(shape):
    nd = len(shape)
    return pl.BlockSpec(shape, lambda *_: (0,) * nd, pipeline_mode=pl.Buffered(1))


def _swiglu_half(h, g_ref, wg_ref, wu_ref, wd_ref):
    ms = jnp.mean(h * h, axis=-1, keepdims=True)
    xn = (h * lax.rsqrt(ms + EPS) * g_ref[...]).astype(BF16)
    gate = _dot(xn, wg_ref[...])
    up = _dot(xn, wu_ref[...])
    act = (gate * (1.0 / (1.0 + jnp.exp(-gate))) * up).astype(BF16)
    return h + 0.5 * _dot(act, wd_ref[...])


def _ffn_kernel(h_ref, g_ref, wg_ref, wu_ref, wd_ref, o_ref):
    o_ref[...] = _swiglu_half(h_ref[...], g_ref, wg_ref, wu_ref, wd_ref)


def _seg_mean_sq(x, ones_ref, seg):
    return _dot((x * x).astype(BF16), ones_ref[...]) * (1.0 / seg)


def _outproj_ffn_kernel(h_ref, om_ref, on_ref, od_ref, gm_ref, gn_ref, gd_ref, ones64_ref, wo_ref,
                        g_ref, wg_ref, wu_ref, wd_ref, o_ref, *, diff_out_scale):
    om = om_ref[...].astype(F32)
    on = on_ref[...].astype(F32)
    od = od_ref[...].astype(F32)
    om = om * lax.rsqrt(jnp.sum(om * om, axis=-1, keepdims=True) * (1.0 / (H_MLA * HEAD_DIM)) + EPS) * gm_ref[...]
    on = on * lax.rsqrt(jnp.sum(on * on, axis=-1, keepdims=True) * (1.0 / (H_NA * HEAD_DIM)) + EPS) * gn_ref[...]
    od = od * lax.rsqrt(_seg_mean_sq(od, ones64_ref, HEAD_DIM) + EPS) * (gd_ref[...] * diff_out_scale)
    cat = jnp.concatenate([om, on, od], axis=-1).astype(BF16)
    h = h_ref[...] + _dot(cat, wo_ref[...])
    o_ref[...] = _swiglu_half(h, g_ref, wg_ref, wu_ref, wd_ref)


def _token_tile(n_tokens, seq):
    tm = min(512, seq)
    assert seq % tm == 0 and n_tokens % tm == 0
    return tm


def _ffn(h, g, wg, wu, wd, seq):
    t, d = h.shape
    tm = _token_tile(t, seq)
    row = pl.BlockSpec((tm, d), lambda i: (i, 0))
    return pl.pallas_call(
        _ffn_kernel,
        grid=(t // tm,),
        in_specs=[row, _const_spec(g.shape), _const_spec(wg.shape), _const_spec(wu.shape), _const_spec(wd.shape)],
        out_specs=row,
        out_shape=jax.ShapeDtypeStruct((t, d), F32),
        compiler_params=pltpu.CompilerParams(dimension_semantics=("parallel",), vmem_limit_bytes=VMEM_LIMIT),
        name="ffn",
    )(h, g, wg, wu, wd)


def _outproj_ffn(h, om, on, od, gm, gn, gd, ones64, wo, g, wg, wu, wd, seq, diff_out_scale):
    t, d = h.shape
    tm = _token_tile(t, seq)
    row = pl.BlockSpec((tm, d), lambda i: (i, 0))
    slot = pl.BlockSpec((tm, W_SLOT), lambda i: (i, 0))
    consts = [gm, gn, gd, ones64, wo, g, wg, wu, wd]
    return pl.pallas_call(
        functools.partial(_outproj_ffn_kernel, diff_out_scale=diff_out_scale),
        grid=(t // tm,),
        in_specs=[row, slot, slot, slot] + [_const_spec(c.shape) for c in consts],
        out_specs=row,
        out_shape=jax.ShapeDtypeStruct((t, d), F32),
        compiler_params=pltpu.CompilerParams(dimension_semantics=("parallel",), vmem_limit_bytes=VMEM_LIMIT),
        name="outproj_ffn",
    )(h, om, on, od, *consts)


def _prep_kernel(h_ref, gmix_ref, win_ref, gq_ref, wq_ref, gkv_ref, wk_ref,
                 gqc_ref, gqs_ref, gkc_ref, gks_ref, cos_ref, sin_ref,
                 ones64_ref, ones32_ref, gnq_ref, gnk_ref, gdq_ref, gdk_ref,
                 qm_ref, km_ref, vm_ref, nq_ref, nk_ref, nv_ref, dq_ref, dk_ref, dv_ref):
    h = h_ref[...]
    ms = jnp.mean(h * h, axis=-1, keepdims=True)
    xn = (h * lax.rsqrt(ms + EPS) * gmix_ref[...]).astype(BF16)
    u = _dot(xn, win_ref[...])

    cq = u[:, 0:Q_LORA]
    cqn = (cq * lax.rsqrt(jnp.mean(cq * cq, axis=-1, keepdims=True) + EPS) * gq_ref[...]).astype(BF16)
    q2 = _dot(cqn, wq_ref[...])
    ckv = u[:, Q_LORA:Q_LORA + KV_LORA]
    ckvn = ckv * lax.rsqrt(jnp.mean(ckv * ckv, axis=-1, keepdims=True) + EPS) * gkv_ref[...]
    kr = u[:, Q_LORA + KV_LORA:Q_LORA + 2 * KV_LORA]
    k2 = _dot(jnp.concatenate([ckvn, kr], axis=-1).astype(BF16), wk_ref[...])

    cos = cos_ref[...]
    sin = sin_ref[...]
    gqc = gqc_ref[...] * cos
    gqs = gqs_ref[...] * sin
    gkc = gkc_ref[...] * cos
    gks = gks_ref[...] * sin
    nsw = H_MLA * LANES
    for hd in range(H_MLA):
        lo, hi = hd * LANES, (hd + 1) * LANES
        qr, qs = q2[:, lo:hi], q2[:, nsw + lo:nsw + hi]
        inv = lax.rsqrt(jnp.sum(qr * qr, axis=-1, keepdims=True) * (1.0 / MLA_QK) + EPS) * (MLA_QK ** -0.5 * LOG2E)
        qm_ref[:, lo:hi] = ((qr * gqc + qs * gqs) * inv).astype(BF16)
        kr_, ks_ = k2[:, lo:hi], k2[:, nsw + lo:nsw + hi]
        inv = lax.rsqrt(jnp.sum(kr_ * kr_, axis=-1, keepdims=True) * (1.0 / MLA_QK) + EPS)
        km_ref[:, lo:hi] = ((kr_ * gkc + ks_ * gks) * inv).astype(BF16)
    vm_ref[0] = k2[:, 2 * nsw:].T.astype(BF16)

    base = 2 * Q_LORA
    naq = u[:, base:base + W_SLOT]
    nak = u[:, base + W_SLOT:base + 2 * W_SLOT]
    nq_ref[...] = (naq * lax.rsqrt(_seg_mean_sq(naq, ones64_ref, HEAD_DIM) + EPS)
                   * (gnq_ref[...] * (HEAD_DIM ** -0.5 * LOG2E))).astype(BF16)
    nk_ref[...] = (nak * lax.rsqrt(_seg_mean_sq(nak, ones64_ref, HEAD_DIM) + EPS) * gnk_ref[...]).astype(BF16)
    nv_ref[0] = u[:, base + 2 * W_SLOT:base + 3 * W_SLOT].T.astype(BF16)
    base += 3 * W_SLOT
    dfq = u[:, base:base + W_SLOT]
    dfk = u[:, base + W_SLOT:base + 2 * W_SLOT]
    dq_ref[...] = (dfq * lax.rsqrt(_seg_mean_sq(dfq, ones32_ref, DIFF_D) + EPS)
                   * (gdq_ref[...] * (DIFF_D ** -0.5 * LOG2E))).astype(BF16)
    dk_ref[...] = (dfk * lax.rsqrt(_seg_mean_sq(dfk, ones32_ref, DIFF_D) + EPS) * gdk_ref[...]).astype(BF16)
    dv_ref[0] = u[:, base + 2 * W_SLOT:base + 3 * W_SLOT].T.astype(BF16)


def _prep(h, consts_a, cos, sin, consts_b, seq):
    t, d = h.shape
    tm = _token_tile(t, seq)
    tiles_per_seq = seq // tm
    row = pl.BlockSpec((tm, d), lambda i: (i, 0))
    pos = pl.BlockSpec((tm, LANES), lambda i: (i % tiles_per_seq, 0))
    widths = [H_MLA * LANES, H_MLA * LANES] + [W_SLOT] * 7
    transposed = [w_i in (2, 5, 8) for w_i in range(len(widths))]
    tok_spec = lambda w: pl.BlockSpec((tm, w), lambda i: (i, 0))
    val_spec = lambda w: pl.BlockSpec((1, w, tm), lambda i: (i // tiles_per_seq, 0, i % tiles_per_seq))
    return pl.pallas_call(
        _prep_kernel,
        grid=(t // tm,),
        in_specs=[row] + [_const_spec(c.shape) for c in consts_a] + [pos, pos]
        + [_const_spec(c.shape) for c in consts_b],
        out_specs=[val_spec(w) if tr else tok_spec(w) for w, tr in zip(widths, transposed)],
        out_shape=[jax.ShapeDtypeStruct((t // seq, w, seq) if tr else (t, w), BF16)
                   for w, tr in zip(widths, transposed)],
        compiler_params=pltpu.CompilerParams(dimension_semantics=("parallel",), vmem_limit_bytes=VMEM_LIMIT),
        name="mix_prep",
    )(h, *consts_a, cos, sin, *consts_b)


def _lane_band(x, lo, width):
    lane = lax.broadcasted_iota(jnp.int32, x.shape, 1)
    return jnp.where((lane >= lo) & (lane < lo + width), x, jnp.zeros_like(x))


def _pipelined_softmax_pv(score_fns, vt_fns, s_bufs, n_keys, chunk, value_rows=HEAD_DIM):
    n_chunk = n_keys // chunk
    rows = lambda c: slice(c * chunk, (c + 1) * chunk)

    def accumulate(u, c, p, acc, l):
        vt = vt_fns[u](c)
        ca = _dot(vt, p.astype(BF16))
        acc = ca if acc is None else acc + ca
        if vt.shape[0] == value_rows:
            cl = jnp.sum(p, axis=0, keepdims=True)
            l = cl if l is None else l + cl
        return acc, l

    finish = lambda acc, l: (acc[:value_rows], acc[value_rows:value_rows + 1] if l is None else l)

    if s_bufs is None:
        results = []
        for u in range(len(score_fns)):
            acc = l = None
            for c in range(n_chunk):
                acc, l = accumulate(u, c, jnp.exp2(score_fns[u](c)), acc, l)
            results.append(finish(acc, l))
        return results

    def score_pass(u, c, m):
        s = score_fns[u](c)
        s_bufs[u % 2][rows(c), :] = s
        cm = jnp.max(s, axis=0, keepdims=True)
        return cm if m is None else jnp.maximum(m, cm)

    m = None
    for c in range(n_chunk):
        m = score_pass(0, c, m)
    results = []
    for u in range(len(score_fns)):
        m_next = acc = l = None
        for c in range(n_chunk):
            if u + 1 < len(score_fns):
                m_next = score_pass(u + 1, c, m_next)
            acc, l = accumulate(u, c, jnp.exp2(s_bufs[u % 2][rows(c), :] - m), acc, l)
        results.append(finish(acc, l))
        m = m_next
    return results


def _store_head_pairs(o_ref, heads_t):
    if len(heads_t) % 2:
        heads_t = heads_t + [jnp.zeros_like(heads_t[0])]
    for p in range(len(heads_t) // 2):
        pair_t = jnp.concatenate(heads_t[2 * p:2 * p + 2], axis=0)
        o_ref[0, :, p * LANES:(p + 1) * LANES] = pair_t.T.astype(o_ref.dtype)


def _mla_attn_kernel(q_ref, k_ref, vt_ref, o_ref, *s_bufs):
    n_keys = k_ref.shape[1]
    chunk = min(KEY_CHUNK, n_keys) if s_bufs else n_keys
    head_cols = lambda h: slice(h * LANES, (h + 1) * LANES)
    key_rows = lambda c: slice(c * chunk, (c + 1) * chunk)
    score_fns = [lambda c, h=h: _dot_nt(k_ref[0, key_rows(c), head_cols(h)], q_ref[0, :, head_cols(h)])
                 for h in range(H_MLA)]
    vt_fns = [lambda c, h=h: vt_ref[0, h * HEAD_DIM:(h + 1) * HEAD_DIM, key_rows(c)] for h in range(H_MLA)]
    res = _pipelined_softmax_pv(score_fns, vt_fns, s_bufs or None, n_keys, chunk)
    _store_head_pairs(o_ref, [acc * (1.0 / l) for acc, l in res])


def _score_scratch(bounded, rows, cols):
    return [] if bounded else [pltpu.VMEM((rows, cols), F32), pltpu.VMEM((rows, cols), F32)]


def _mla_attn(q, k, vt, tq, bounded):
    b, s, _ = q.shape
    return pl.pallas_call(
        _mla_attn_kernel,
        grid=(b, s // tq),
        in_specs=[pl.BlockSpec((1, tq, H_MLA * LANES), lambda bi, qi: (bi, qi, 0)),
                  pl.BlockSpec((1, s, H_MLA * LANES), lambda bi, qi: (bi, 0, 0)),
                  pl.BlockSpec((1, W_SLOT, s), lambda bi, qi: (bi, 0, 0))],
        out_specs=pl.BlockSpec((1, tq, W_SLOT), lambda bi, qi: (bi, qi, 0)),
        out_shape=jax.ShapeDtypeStruct((b, s, W_SLOT), BF16),
        scratch_shapes=_score_scratch(bounded, s, tq),
        compiler_params=pltpu.CompilerParams(dimension_semantics=("parallel", "parallel"),
                                             vmem_limit_bytes=VMEM_LIMIT),
        name="mla_attn",
    )(q, k, vt)


def _diff_attn_kernel(lam_ref, q_ref, k_ref, vt_ref, bias_ref, o_ref, *s_bufs, vt_rows):
    lam = lam_ref[0]
    n_keys = k_ref.shape[1]
    chunk = min(KEY_CHUNK, n_keys) if s_bufs else n_keys
    key_rows = lambda c: slice(c * chunk, (c + 1) * chunk)
    score_fns, vt_fns = [], []
    for h in range(H_DIFF):
        slot = slice((h // 2) * LANES, (h // 2 + 1) * LANES)
        for m in range(2):
            qm = _lane_band(q_ref[0, :, slot], (2 * (h % 2) + m) * DIFF_D, DIFF_D)
            score_fns.append(lambda c, h=h, slot=slot, qm=qm:
                             _dot_nt(k_ref[0, key_rows(c), slot], qm) + bias_ref[h, key_rows(c), :])
            vt_fns.append(lambda c, h=h: vt_ref[0, h * vt_rows:(h + 1) * vt_rows, key_rows(c)])
    res = _pipelined_softmax_pv(score_fns, vt_fns, s_bufs or None, n_keys, chunk)
    heads_t = []
    for h in range(H_DIFF):
        (a1, l1), (a2, l2) = res[2 * h], res[2 * h + 1]
        heads_t.append(a1 * (1.0 / l1) - a2 * (lam / l2))
    _store_head_pairs(o_ref, heads_t)


def _diff_attn(lam, q, k, vt, bias_t, tq, bounded):
    b, s, _ = q.shape
    return pl.pallas_call(
        functools.partial(_diff_attn_kernel, vt_rows=HEAD_DIM if bounded else VT_ROWS),
        grid=(s // tq, b),
        in_specs=[pl.BlockSpec(memory_space=pltpu.SMEM),
                  pl.BlockSpec((1, tq, W_SLOT), lambda qi, bi: (bi, qi, 0)),
                  pl.BlockSpec((1, s, W_SLOT), lambda qi, bi: (bi, 0, 0)),
                  pl.BlockSpec((1, vt.shape[1], s), lambda qi, bi: (bi, 0, 0)),
                  pl.BlockSpec((H_DIFF, s, tq), lambda qi, bi: (0, 0, qi), pipeline_mode=pl.Buffered(1))],
        out_specs=pl.BlockSpec((1, tq, W_SLOT), lambda qi, bi: (bi, qi, 0)),
        out_shape=jax.ShapeDtypeStruct((b, s, W_SLOT), BF16),
        scratch_shapes=_score_scratch(bounded, s, tq),
        compiler_params=pltpu.CompilerParams(dimension_semantics=("parallel", "parallel"),
                                             vmem_limit_bytes=VMEM_LIMIT),
        name="diff_attn",
    )(lam, q, k, vt, bias_t)


def _na_plan(seq):
    rows = seq // GRID_W
    kr = min(NA_KR_MAX, rows)
    assert rows % NA_QROWS == 0 and rows >= NA_WROWS
    patterns, blocks = [], []
    for r0 in range(0, rows, NA_QROWS):
        ws = int(np.clip(r0 - kr // 2, 0, rows - NA_WROWS))
        rs = [int(np.clip(r0 + j - kr // 2, 0, rows - kr)) for j in range(NA_QROWS)]
        assert all(ws <= x and x + kr <= ws + NA_WROWS for x in rs)
        pat = (r0 - ws, tuple(x - ws for x in rs))
        if pat not in patterns:
            patterns.append(pat)
        blocks.append((r0, ws, patterns.index(pat)))
    return kr, patterns, blocks


def _na_bias_table(rpb, seq):
    kr, patterns, _ = _na_plan(seq)
    nq, nk = NA_QROWS * GRID_W, NA_WROWS * GRID_W
    c = np.arange(GRID_W)
    cstart = np.clip(c - NA_KC // 2, 0, GRID_W - NA_KC)
    col_ok = (c[None, :] >= cstart[:, None]) & (c[None, :] < cstart[:, None] + NA_KC)
    col_sel = (c[None, :, None] - c[:, None, None] + NA_KC - 1) == np.arange(2 * NA_KC - 1)
    j, i = np.arange(NA_QROWS), np.arange(NA_WROWS)
    row_ok, row_sel = [], []
    for d0, rs_rel in patterns:
        rs = np.asarray(rs_rel)
        row_ok.append((i[None, :] >= rs[:, None]) & (i[None, :] < rs[:, None] + kr))
        row_sel.append((i[None, :, None] - (d0 + j[:, None, None]) + NA_KR_MAX - 1) == np.arange(2 * NA_KR_MAX - 1))
    row_ok, row_sel = np.stack(row_ok), np.stack(row_sel)
    rows = jnp.einsum('pjia,hab->hpjib', jnp.asarray(row_sel, F32), rpb * LOG2E, precision=lax.Precision.HIGHEST)
    vals = jnp.einsum('hpjib,ckb->hpikjc', rows, jnp.asarray(col_sel, F32), precision=lax.Precision.HIGHEST)
    ok = row_ok.transpose(0, 2, 1)[:, :, None, :, None] & col_ok.T[None, None, :, None, :]
    tab = jnp.where(jnp.asarray(ok)[None], vals, MASK_VALUE).reshape(H_NA, len(patterns), nk, nq)
    tab = jnp.concatenate([tab, jnp.zeros((2 * N_PAIR - H_NA,) + tab.shape[1:], F32)], axis=0)
    return tab.reshape(N_PAIR, 2, len(patterns), nk, nq)


def _na_attn_kernel(q_ref, k_ref, vt_ref, bias_ref, o_ref, *s_bufs, blocks, vt_rows):
    nq, nk = NA_QROWS * GRID_W, NA_WROWS * GRID_W
    score_fns, vt_fns, places = [], [], []
    for r0, ws, pat in blocks:
        q_rows = slice(r0 * GRID_W, r0 * GRID_W + nq)
        k_rows = slice(ws * GRID_W, ws * GRID_W + nk)
        for p in range(N_PAIR):
            slot = slice(p * LANES, (p + 1) * LANES)

            def score(c, q_rows=q_rows, k_rows=k_rows, slot=slot, p=p, pat=pat):
                q = q_ref[0, q_rows, slot]
                q2 = jnp.concatenate([_lane_band(q, 0, HEAD_DIM), _lane_band(q, HEAD_DIM, HEAD_DIM)], axis=0)
                bias = jnp.concatenate([bias_ref[p, 0, pat], bias_ref[p, 1, pat]], axis=1)
                return _dot_nt(k_ref[0, k_rows, slot], q2) + bias

            score_fns.append(score)
            vt_fns.append(lambda c, k_rows=k_rows, p=p: vt_ref[0, p * vt_rows:(p + 1) * vt_rows, k_rows])
            places.append((q_rows, slot))
    res = _pipelined_softmax_pv(score_fns, vt_fns, s_bufs or None, nk, nk, value_rows=LANES)
    for (acc, l), (q_rows, slot) in zip(res, places):
        o = acc * (1.0 / l)
        pair_t = jnp.concatenate([o[:HEAD_DIM, :nq], o[HEAD_DIM:, nq:]], axis=0)
        o_ref[0, q_rows, slot] = pair_t.T.astype(o_ref.dtype)


def _na_attn(q, k, vt, bias_tab, bounded):
    b, s, _ = q.shape
    _, _, blocks = _na_plan(s)
    nq, nk = NA_QROWS * GRID_W, NA_WROWS * GRID_W
    tok = pl.BlockSpec((1, s, W_SLOT), lambda bi: (bi, 0, 0))
    return pl.pallas_call(
        functools.partial(_na_attn_kernel, blocks=tuple(blocks), vt_rows=LANES if bounded else NA_VT_ROWS),
        grid=(b,),
        in_specs=[tok, tok, pl.BlockSpec((1, vt.shape[1], s), lambda bi: (bi, 0, 0)),
                  _const_spec(bias_tab.shape)],
        out_specs=tok,
        out_shape=jax.ShapeDtypeStruct((b, s, W_SLOT), BF16),
        scratch_shapes=_score_scratch(bounded, nk, 2 * nq),
        compiler_params=pltpu.CompilerParams(dimension_semantics=("parallel",), vmem_limit_bytes=VMEM_LIMIT),
        name="na_attn",
    )(q, k, vt, bias_tab)


def _pad_cols(w, groups):
    parts = []
    for start, width, padded in groups:
        parts.append(w[..., start:start + width])
        if padded > width:
            parts.append(jnp.zeros(w.shape[:-1] + (padded - width,), w.dtype))
    return jnp.concatenate(parts, axis=-1)


def _t5_bucket(rel):
    half = T5_BUCKETS // 2
    max_exact = half // 2
    n = jnp.abs(rel)
    nf = jnp.maximum(n, 1).astype(F32)
    large = max_exact + (jnp.log(nf / max_exact) / math.log(T5_MAX_DIST / max_exact)
                         * (half - max_exact)).astype(jnp.int32)
    large = jnp.minimum(large, half - 1)
    return jnp.where(rel > 0, half, 0) + jnp.where(n < max_exact, n, large)


def _t5_bias_kernel(thr_ref, tab_ref, o_ref):
    half = T5_BUCKETS // 2
    head = pl.program_id(0)
    tk, s = o_ref.shape[1:]
    k = pl.program_id(1) * tk + lax.broadcasted_iota(jnp.int32, (tk, s), 0)
    rel = k - lax.broadcasted_iota(jnp.int32, (tk, s), 1)
    n = jnp.abs(rel)
    neg = jnp.full((tk, s), tab_ref[head * T5_BUCKETS], F32)
    pos = jnp.full((tk, s), tab_ref[head * T5_BUCKETS + half], F32)
    for j in range(1, half):
        ge = n >= thr_ref[j]
        neg = jnp.where(ge, tab_ref[head * T5_BUCKETS + j], neg)
        pos = jnp.where(ge, tab_ref[head * T5_BUCKETS + half + j], pos)
    o_ref[0] = jnp.where(rel > 0, pos, neg)


def _t5_bias(t5_table, seq, tq):
    half = T5_BUCKETS // 2
    dist_bucket = _t5_bucket(-jnp.arange(seq, dtype=jnp.int32))
    thr = jnp.sum(dist_bucket[None, :] < jnp.arange(half, dtype=jnp.int32)[:, None], axis=1).astype(jnp.int32)
    tab = t5_table.T * LOG2E
    return pl.pallas_call(
        _t5_bias_kernel,
        grid=(H_DIFF, seq // tq),
        in_specs=[pl.BlockSpec(memory_space=pltpu.SMEM), pl.BlockSpec(memory_space=pltpu.SMEM)],
        out_specs=pl.BlockSpec((1, tq, seq), lambda h, ki: (h, ki, 0)),
        out_shape=jax.ShapeDtypeStruct((H_DIFF, seq, seq), F32),
        compiler_params=pltpu.CompilerParams(dimension_semantics=("parallel", "parallel"),
                                             vmem_limit_bytes=VMEM_LIMIT),
        name="t5_bias",
    )(thr, tab.reshape(-1))


def _swap_rope_lanes(w):
    r = MLA_ROPE // 2
    return jnp.concatenate([w[..., :MLA_NOPE], w[..., MLA_NOPE + r:], w[..., MLA_NOPE:MLA_NOPE + r]], axis=-1)


def _pad_lanes(w, width=LANES):
    return jnp.concatenate([w, jnp.zeros(w.shape[:-1] + (width - w.shape[-1],), w.dtype)], axis=-1)


def _block_ones(width, seg):
    idx = np.arange(width) // seg
    return jnp.asarray(idx[:, None] == idx[None, :], dtype=BF16)


def kernel(x, ffn1_norm, ffn1_w_gate, ffn1_w_up, ffn1_w_down, mix_norm, w_in, mla_q_norm, mla_w_uq, mla_kv_norm, mla_w_ukv, mla_qk_q_norm, mla_qk_k_norm, mla_out_norm, na_qk_q_norm, na_qk_k_norm, na_rpb, na_out_norm, diff_qk_q_norm, diff_qk_k_norm, diff_lambda_q1, diff_lambda_k1, diff_lambda_q2, diff_lambda_k2, diff_subln, t5_table, w_out, ffn2_norm, ffn2_w_gate, ffn2_w_up, ffn2_w_down):
    b, s, d = x.shape
    depth = w_in.shape[0]
    t = b * s
    tq = min(512, s)

    inv_freq = ROPE_BASE ** (-jnp.arange(0, MLA_ROPE, 2, dtype=F32) / MLA_ROPE)
    ang = jnp.arange(s, dtype=F32)[:, None] * inv_freq[None, :]
    cos, sin = jnp.cos(ang), jnp.sin(ang)
    cos_tab = jnp.concatenate([jnp.ones((s, MLA_NOPE), F32), cos, cos, jnp.zeros((s, LANES - MLA_QK), F32)], axis=-1)
    sin_tab = jnp.concatenate([jnp.zeros((s, MLA_NOPE), F32), -sin, sin, jnp.zeros((s, LANES - MLA_QK), F32)], axis=-1)

    t5_bias = _t5_bias(t5_table, s, tq)
    ones64 = _block_ones(W_SLOT, HEAD_DIM)
    ones32 = _block_ones(W_SLOT, DIFF_D)
    seg = H_NA * HEAD_DIM
    in_groups = [(0, Q_LORA + KV_LORA + MLA_ROPE, 2 * Q_LORA)] + [
        (Q_LORA + KV_LORA + MLA_ROPE + i * seg, seg, W_SLOT) for i in range(6)]

    h = x.reshape(t, d)
    for l in range(depth):
        row = lambda v: v[l][None, :]
        h = _ffn(h, row(ffn1_norm), ffn1_w_gate[l].astype(BF16), ffn1_w_up[l].astype(BF16),
                 ffn1_w_down[l].astype(BF16), s)

        win = _pad_cols(w_in[l], in_groups).astype(BF16)
        wuq = mla_w_uq[l].reshape(Q_LORA, H_MLA, MLA_QK)
        wq_all = jnp.concatenate([_pad_lanes(wuq).reshape(Q_LORA, -1),
                                  _pad_lanes(_swap_rope_lanes(wuq)).reshape(Q_LORA, -1)], axis=-1).astype(BF16)
        wukv = mla_w_ukv[l].reshape(KV_LORA, H_MLA, MLA_NOPE + HEAD_DIM)
        eye = jnp.broadcast_to(jnp.eye(MLA_ROPE, dtype=F32)[:, None, :], (MLA_ROPE, H_MLA, MLA_ROPE))
        k_top = jnp.concatenate([wukv[..., :MLA_NOPE], jnp.zeros((KV_LORA, H_MLA, MLA_ROPE), F32)], axis=-1)
        k_bot = jnp.concatenate([jnp.zeros((MLA_ROPE, H_MLA, MLA_NOPE), F32), eye], axis=-1)
        k_rows = jnp.concatenate([k_top, k_bot, jnp.zeros((KV_LORA - MLA_ROPE, H_MLA, MLA_QK), F32)], axis=0)
        v_rows = jnp.concatenate([wukv[..., MLA_NOPE:].reshape(KV_LORA, -1),
                                  jnp.zeros((KV_LORA, H_MLA * HEAD_DIM), F32)], axis=0)
        wk_all = jnp.concatenate([_pad_lanes(k_rows).reshape(2 * KV_LORA, -1),
                                  _pad_lanes(_swap_rope_lanes(k_rows)).reshape(2 * KV_LORA, -1),
                                  v_rows], axis=-1).astype(BF16)
        gq, gk = mla_qk_q_norm[l], mla_qk_k_norm[l]
        tile_pairs = lambda g, n: _pad_lanes(jnp.tile(g, n), W_SLOT)[None, :]
        consts_a = [row(mix_norm), win, row(mla_q_norm), wq_all, row(mla_kv_norm), wk_all,
                    _pad_lanes(gq)[None, :], _pad_lanes(_swap_rope_lanes(gq))[None, :],
                    _pad_lanes(gk)[None, :], _pad_lanes(_swap_rope_lanes(gk))[None, :]]
        consts_b = [ones64, ones32, tile_pairs(na_qk_q_norm[l], H_NA), tile_pairs(na_qk_k_norm[l], H_NA),
                    tile_pairs(diff_qk_q_norm[l], 2 * H_DIFF), tile_pairs(diff_qk_k_norm[l], 2 * H_DIFF)]
        qm, km, vm, nq, nk, nv, dq, dk, dv = _prep(h, consts_a, cos_tab, sin_tab, consts_b, s)
        seq3 = lambda a: a.reshape(b, s, a.shape[-1])


        def groups_with_ones(vt, n_groups, width):
            vg = vt[:, :n_groups * width].reshape(b, n_groups, width, s)
            vg = jnp.concatenate([vg, jnp.ones((b, n_groups, 16, s), BF16)], axis=2)
            return vg.reshape(b, n_groups * (width + 16), s)

        lam_init = 0.8 - 0.6 * math.exp(-0.3 * l)
        lam = (jnp.exp(jnp.sum(diff_lambda_q1[l] * diff_lambda_k1[l]))
               - jnp.exp(jnp.sum(diff_lambda_q2[l] * diff_lambda_k2[l])) + lam_init).reshape(1)
        na_bias = _na_bias_table(na_rpb[l], s)

        def attention(bounded):
            if bounded:
                vt_na, vt_diff = nv, dv
            else:
                vt_na = groups_with_ones(nv, N_PAIR, LANES)
                vt_diff = groups_with_ones(dv, H_DIFF, HEAD_DIM)
            return (_mla_attn(seq3(qm), seq3(km), vm, tq, bounded),
                    _na_attn(seq3(nq), seq3(nk), vt_na, na_bias, bounded),
                    _diff_attn(lam, seq3(dq), seq3(dk), vt_diff, t5_bias, tq, bounded))

        amax = lambda v: jnp.max(jnp.abs(v))
        logit_bound = LOG2E * jnp.maximum(jnp.maximum(
            math.sqrt(MLA_QK) * amax(mla_qk_q_norm[l]) * amax(mla_qk_k_norm[l]),
            math.sqrt(HEAD_DIM) * amax(na_qk_q_norm[l]) * amax(na_qk_k_norm[l]) + amax(na_rpb[l])),
            math.sqrt(DIFF_D) * amax(diff_qk_q_norm[l]) * amax(diff_qk_k_norm[l]) + amax(t5_table))
        o_mla, o_na, o_diff = lax.cond(logit_bound < BOUNDED_LOGIT_LIMIT,
                                       lambda: attention(True), lambda: attention(False))

        wo = w_out[l]
        wo_all = jnp.concatenate([wo[:H_MLA * HEAD_DIM],
                                  wo[H_MLA * HEAD_DIM:H_MLA * HEAD_DIM + seg], jnp.zeros((W_SLOT - seg, d), F32),
                                  wo[H_MLA * HEAD_DIM + seg:], jnp.zeros((W_SLOT - seg, d), F32)], axis=0).astype(BF16)
        h = _outproj_ffn(h, o_mla.reshape(t, -1), o_na.reshape(t, -1), o_diff.reshape(t, -1),
                         row(mla_out_norm), _pad_lanes(na_out_norm[l], W_SLOT)[None, :],
                         tile_pairs(diff_subln[l], H_DIFF), ones64, wo_all,
                         row(ffn2_norm), ffn2_w_gate[l].astype(BF16), ffn2_w_up[l].astype(BF16),
                         ffn2_w_down[l].astype(BF16), s, 1.0 - lam_init)
    return h.reshape(b, s, d)
```

```python
import functools
import math

import numpy as np
import jax
import jax.numpy as jnp
from jax import lax
from jax.experimental import pallas as pl
from jax.experimental.pallas import tpu as pltpu

HEAD_DIM = 64
H_MLA, H_NA, H_DIFF = 6, 5, 5
Q_LORA, KV_LORA = 256, 128
MLA_NOPE, MLA_ROPE = 64, 32
MLA_QK = MLA_NOPE + MLA_ROPE
ROPE_BASE = 10000.0
GRID_W = 64
NA_KR_MAX, NA_KC = 8, 16
DIFF_D = 32
T5_BUCKETS, T5_MAX_DIST = 32, 128
EPS = 1e-6

LANES = 128
N_PAIR = 3
W_SLOT = N_PAIR * LANES
NA_QROWS = 4
KEY_CHUNK = 1024
VT_ROWS = HEAD_DIM + 16
NA_VT_ROWS = LANES + 16
NA_WROWS = 12
MASK_VALUE = -1e30
LOG2E = math.log2(math.e)
BOUNDED_LOGIT_LIMIT = 60.0
VMEM_LIMIT = 56 * 1024 * 1024

BF16 = jnp.bfloat16
F32 = jnp.float32


def _dot(a, b):
    return jnp.dot(a, b, preferred_element_type=F32)


def _dot_nt(a, b):
    return lax.dot_general(a, b, (((1,), (1,)), ((), ())), preferred_element_type=F32)


def _const_spec(shape):
    nd = len(shape)
    return pl.BlockSpec(shape, lambda *_: (0,) * nd, pipeline_mode=pl.Buffered(1))


def _swiglu_half(h, g_ref, wg_ref, wu_ref, wd_ref):
    ms = jnp.mean(h * h, axis=-1, keepdims=True)
    xn = (h * lax.rsqrt(ms + EPS) * g_ref[...]).astype(BF16)
    gate = _dot(xn, wg_ref[...])
    up = _dot(xn, wu_ref[...])
    act = (gate * (1.0 / (1.0 + jnp.exp(-gate))) * up).astype(BF16)
    return h + 0.5 * _dot(act, wd_ref[...])


def _ffn_kernel(h_ref, g_ref, wg_ref, wu_ref, wd_ref, o_ref):
    o_ref[...] = _swiglu_half(h_ref[...], g_ref, wg_ref, wu_ref, wd_ref)


def _seg_mean_sq(x, ones_ref, seg):
    return _dot((x * x).astype(BF16), ones_ref[...]) * (1.0 / seg)


def _outproj_ffn_kernel(h_ref, om_ref, on_ref, od_ref, gm_ref, gn_ref, gd_ref, ones64_ref, wo_ref,
                        g_ref, wg_ref, wu_ref, wd_ref, o_ref, *, diff_out_scale):
    om = om_ref[...].astype(F32)
    on = on_ref[...].astype(F32)
    od = od_ref[...].astype(F32)
    om = om * lax.rsqrt(jnp.sum(om * om, axis=-1, keepdims=True) * (1.0 / (H_MLA * HEAD_DIM)) + EPS) * gm_ref[...]
    on = on * lax.rsqrt(jnp.sum(on * on, axis=-1, keepdims=True) * (1.0 / (H_NA * HEAD_DIM)) + EPS) * gn_ref[...]
    od = od * lax.rsqrt(_seg_mean_sq(od, ones64_ref, HEAD_DIM) + EPS) * (gd_ref[...] * diff_out_scale)
    cat = jnp.concatenate([om, on, od], axis=-1).astype(BF16)
    h = h_ref[...] + _dot(cat, wo_ref[...])
    o_ref[...] = _swiglu_half(h, g_ref, wg_ref, wu_ref, wd_ref)


def _token_tile(n_tokens, seq):
    tm = min(512, seq)
    assert seq % tm == 0 and n_tokens % tm == 0
    return tm


def _ffn(h, g, wg, wu, wd, seq):
    t, d = h.shape
    tm = _token_tile(t, seq)
    row = pl.BlockSpec((tm, d), lambda i: (i, 0))
    return pl.pallas_call(
        _ffn_kernel,
        grid=(t // tm,),
        in_specs=[row, _const_spec(g.shape), _const_spec(wg.shape), _const_spec(wu.shape), _const_spec(wd.shape)],
        out_specs=row,
        out_shape=jax.ShapeDtypeStruct((t, d), F32),
        compiler_params=pltpu.CompilerParams(dimension_semantics=("parallel",), vmem_limit_bytes=VMEM_LIMIT),
        name="ffn",
    )(h, g, wg, wu, wd)


def _outproj_ffn(h, om, on, od, gm, gn, gd, ones64, wo, g, wg, wu, wd, seq, diff_out_scale):
    t, d = h.shape
    tm = _token_tile(t, seq)
    row = pl.BlockSpec((tm, d), lambda i: (i, 0))
    slot = pl.BlockSpec((tm, W_SLOT), lambda i: (i, 0))
    consts = [gm, gn, gd, ones64, wo, g, wg, wu, wd]
    return pl.pallas_call(
        functools.partial(_outproj_ffn_kernel, diff_out_scale=diff_out_scale),
        grid=(t // tm,),
        in_specs=[row, slot, slot, slot] + [_const_spec(c.shape) for c in consts],
        out_specs=row,
        out_shape=jax.ShapeDtypeStruct((t, d), F32),
        compiler_params=pltpu.CompilerParams(dimension_semantics=("parallel",), vmem_limit_bytes=VMEM_LIMIT),
        name="outproj_ffn",
    )(h, om, on, od, *consts)


def _prep_kernel(h_ref, gmix_ref, win_ref, gq_ref, wq_ref, gkv_ref, wk_ref,
                 gqc_ref, gqs_ref, gkc_ref, gks_ref, cos_ref, sin_ref,
                 ones64_ref, ones32_ref, gnq_ref, gnk_ref, gdq_ref, gdk_ref,
                 qm_ref, km_ref, vm_ref, nq_ref, nk_ref, nv_ref, dq_ref, dk_ref, dv_ref):
    h = h_ref[...]
    ms = jnp.mean(h * h, axis=-1, keepdims=True)
    xn = (h * lax.rsqrt(ms + EPS) * gmix_ref[...]).astype(BF16)
    u = _dot(xn, win_ref[...])

    cq = u[:, 0:Q_LORA]
    cqn = (cq * lax.rsqrt(jnp.mean(cq * cq, axis=-1, keepdims=True) + EPS) * gq_ref[...]).astype(BF16)
    q2 = _dot(cqn, wq_ref[...])
    ckv = u[:, Q_LORA:Q_LORA + KV_LORA]
    ckvn = ckv * lax.rsqrt(jnp.mean(ckv * ckv, axis=-1, keepdims=True) + EPS) * gkv_ref[...]
    kr = u[:, Q_LORA + KV_LORA:Q_LORA + 2 * KV_LORA]
    k2 = _dot(jnp.concatenate([ckvn, kr], axis=-1).astype(BF16), wk_ref[...])

    cos = cos_ref[...]
    sin = sin_ref[...]
    gqc = gqc_ref[...] * cos
    gqs = gqs_ref[...] * sin
    gkc = gkc_ref[...] * cos
    gks = gks_ref[...] * sin
    nsw = H_MLA * LANES
    for hd in range(H_MLA):
        lo, hi = hd * LANES, (hd + 1) * LANES
        qr, qs = q2[:, lo:hi], q2[:, nsw + lo:nsw + hi]
        inv = lax.rsqrt(jnp.sum(qr * qr, axis=-1, keepdims=True) * (1.0 / MLA_QK) + EPS) * (MLA_QK ** -0.5 * LOG2E)
        qm_ref[:, lo:hi] = ((qr * gqc + qs * gqs) * inv).astype(BF16)
        kr_, ks_ = k2[:, lo:hi], k2[:, nsw + lo:nsw + hi]
        inv = lax.rsqrt(jnp.sum(kr_ * kr_, axis=-1, keepdims=True) * (1.0 / MLA_QK) + EPS)
        km_ref[:, lo:hi] = ((kr_ * gkc + ks_ * gks) * inv).astype(BF16)
    vm_ref[0] = k2[:, 2 * nsw:].T.astype(BF16)

    base = 2 * Q_LORA
    naq = u[:, base:base + W_SLOT]
    nak = u[:, base + W_SLOT:base + 2 * W_SLOT]
    nq_ref[...] = (naq * lax.rsqrt(_seg_mean_sq(naq, ones64_ref, HEAD_DIM) + EPS)
                   * (gnq_ref[...] * (HEAD_DIM ** -0.5 * LOG2E))).astype(BF16)
    nk_ref[...] = (nak * lax.rsqrt(_seg_mean_sq(nak, ones64_ref, HEAD_DIM) + EPS) * gnk_ref[...]).astype(BF16)
    nv_ref[0] = u[:, base + 2 * W_SLOT:base + 3 * W_SLOT].T.astype(BF16)
    base += 3 * W_SLOT
    dfq = u[:, base:base + W_SLOT]
    dfk = u[:, base + W_SLOT:base + 2 * W_SLOT]
    dq_ref[...] = (dfq * lax.rsqrt(_seg_mean_sq(dfq, ones32_ref, DIFF_D) + EPS)
                   * (gdq_ref[...] * (DIFF_D ** -0.5 * LOG2E))).astype(BF16)
    dk_ref[...] = (dfk * lax.rsqrt(_seg_mean_sq(dfk, ones32_ref, DIFF_D) + EPS) * gdk_ref[...]).astype(BF16)
    dv_ref[0] = u[:, base + 2 * W_SLOT:base + 3 * W_SLOT].T.astype(BF16)


def _prep(h, consts_a, cos, sin, consts_b, seq):
    t, d = h.shape
    tm = _token_tile(t, seq)
    tiles_per_seq = seq // tm
    row = pl.BlockSpec((tm, d), lambda i: (i, 0))
    pos = pl.BlockSpec((tm, LANES), lambda i: (i % tiles_per_seq, 0))
    widths = [H_MLA * LANES, H_MLA * LANES] + [W_SLOT] * 7
    transposed = [w_i in (2, 5, 8) for w_i in range(len(widths))]
    tok_spec = lambda w: pl.BlockSpec((tm, w), lambda i: (i, 0))
    val_spec = lambda w: pl.BlockSpec((1, w, tm), lambda i: (i // tiles_per_seq, 0, i % tiles_per_seq))
    return pl.pallas_call(
        _prep_kernel,
        grid=(t // tm,),
        in_specs=[row] + [_const_spec(c.shape) for c in consts_a] + [pos, pos]
        + [_const_spec(c.shape) for c in consts_b],
        out_specs=[val_spec(w) if tr else tok_spec(w) for w, tr in zip(widths, transposed)],
        out_shape=[jax.ShapeDtypeStruct((t // seq, w, seq) if tr else (t, w), BF16)
                   for w, tr in zip(widths, transposed)],
        compiler_params=pltpu.CompilerParams(dimension_semantics=("parallel",), vmem_limit_bytes=VMEM_LIMIT),
        name="mix_prep",
    )(h, *consts_a, cos, sin, *consts_b)


def _lane_band(x, lo, width):
    lane = lax.broadcasted_iota(jnp.int32, x.shape, 1)
    return jnp.where((lane >= lo) & (lane < lo + width), x, jnp.zeros_like(x))


def _pipelined_softmax_pv(score_fns, vt_fns, s_bufs, n_keys, chunk, value_rows=HEAD_DIM):
    n_chunk = n_keys // chunk
    rows = lambda c: slice(c * chunk, (c + 1) * chunk)

    def accumulate(u, c, p, acc, l):
        vt = vt_fns[u](c)
        ca = _dot(vt, p.astype(BF16))
        acc = ca if acc is None else acc + ca
        if vt.shape[0] == value_rows:
            cl = jnp.sum(p, axis=0, keepdims=True)
            l = cl if l is None else l + cl
        return acc, l

    finish = lambda acc, l: (acc[:value_rows], acc[value_rows:value_rows + 1] if l is None else l)

    if s_bufs is None:
        results = []
        for u in range(len(score_fns)):
            acc = l = None
            for c in range(n_chunk):
                acc, l = accumulate(u, c, jnp.exp2(score_fns[u](c)), acc, l)
            results.append(finish(acc, l))
        return results

    def score_pass(u, c, m):
        s = score_fns[u](c)
        s_bufs[u % 2][rows(c), :] = s
        cm = jnp.max(s, axis=0, keepdims=True)
        return cm if m is None else jnp.maximum(m, cm)

    m = None
    for c in range(n_chunk):
        m = score_pass(0, c, m)
    results = []
    for u in range(len(score_fns)):
        m_next = acc = l = None
        for c in range(n_chunk):
            if u + 1 < len(score_fns):
                m_next = score_pass(u + 1, c, m_next)
            acc, l = accumulate(u, c, jnp.exp2(s_bufs[u % 2][rows(c), :] - m), acc, l)
        results.append(finish(acc, l))
        m = m_next
    return results


def _store_head_pairs(o_ref, heads_t):
    if len(heads_t) % 2:
        heads_t = heads_t + [jnp.zeros_like(heads_t[0])]
    for p in range(len(heads_t) // 2):
        pair_t = jnp.concatenate(heads_t[2 * p:2 * p + 2], axis=0)
        o_ref[0, :, p * LANES:(p + 1) * LANES] = pair_t.T.astype(o_ref.dtype)


def _mla_attn_kernel(q_ref, k_ref, vt_ref, o_ref, *s_bufs):
    n_keys = k_ref.shape[1]
    chunk = min(KEY_CHUNK, n_keys) if s_bufs else n_keys
    head_cols = lambda h: slice(h * LANES, (h + 1) * LANES)
    key_rows = lambda c: slice(c * chunk, (c + 1) * chunk)
    score_fns = [lambda c, h=h: _dot_nt(k_ref[0, key_rows(c), head_cols(h)], q_ref[0, :, head_cols(h)])
                 for h in range(H_MLA)]
    vt_fns = [lambda c, h=h: vt_ref[0, h * HEAD_DIM:(h + 1) * HEAD_DIM, key_rows(c)] for h in range(H_MLA)]
    res = _pipelined_softmax_pv(score_fns, vt_fns, s_bufs or None, n_keys, chunk)
    _store_head_pairs(o_ref, [acc * (1.0 / l) for acc, l in res])


def _score_scratch(bounded, rows, cols):
    return [] if bounded else [pltpu.VMEM((rows, cols), F32), pltpu.VMEM((rows, cols), F32)]


def _mla_attn(q, k, vt, tq, bounded):
    b, s, _ = q.shape
    return pl.pallas_call(
        _mla_attn_kernel,
        grid=(b, s // tq),
        in_specs=[pl.BlockSpec((1, tq, H_MLA * LANES), lambda bi, qi: (bi, qi, 0)),
                  pl.BlockSpec((1, s, H_MLA * LANES), lambda bi, qi: (bi, 0, 0)),
                  pl.BlockSpec((1, W_SLOT, s), lambda bi, qi: (bi, 0, 0))],
        out_specs=pl.BlockSpec((1, tq, W_SLOT), lambda bi, qi: (bi, qi, 0)),
        out_shape=jax.ShapeDtypeStruct((b, s, W_SLOT), BF16),
        scratch_shapes=_score_scratch(bounded, s, tq),
        compiler_params=pltpu.CompilerParams(dimension_semantics=("parallel", "parallel"),
                                             vmem_limit_bytes=VMEM_LIMIT),
        name="mla_attn",
    )(q, k, vt)


def _diff_attn_kernel(lam_ref, q_ref, k_ref, vt_ref, bias_ref, o_ref, *s_bufs, vt_rows):
    lam = lam_ref[0]
    n_keys = k_ref.shape[1]
    chunk = min(KEY_CHUNK, n_keys) if s_bufs else n_keys
    key_rows = lambda c: slice(c * chunk, (c + 1) * chunk)
    score_fns, vt_fns = [], []
    for h in range(H_DIFF):
        slot = slice((h // 2) * LANES, (h // 2 + 1) * LANES)
        for m in range(2):
            qm = _lane_band(q_ref[0, :, slot], (2 * (h % 2) + m) * DIFF_D, DIFF_D)
            score_fns.append(lambda c, h=h, slot=slot, qm=qm:
                             _dot_nt(k_ref[0, key_rows(c), slot], qm) + bias_ref[h, key_rows(c), :])
            vt_fns.append(lambda c, h=h: vt_ref[0, h * vt_rows:(h + 1) * vt_rows, key_rows(c)])
    res = _pipelined_softmax_pv(score_fns, vt_fns, s_bufs or None, n_keys, chunk)
    heads_t = []
    for h in range(H_DIFF):
        (a1, l1), (a2, l2) = res[2 * h], res[2 * h + 1]
        heads_t.append(a1 * (1.0 / l1) - a2 * (lam / l2))
    _store_head_pairs(o_ref, heads_t)


def _diff_attn(lam, q, k, vt, bias_t, tq, bounded):
    b, s, _ = q.shape
    return pl.pallas_call(
        functools.partial(_diff_attn_kernel, vt_rows=HEAD_DIM if bounded else VT_ROWS),
        grid=(s // tq, b),
        in_specs=[pl.BlockSpec(memory_space=pltpu.SMEM),
                  pl.BlockSpec((1, tq, W_SLOT), lambda qi, bi: (bi, qi, 0)),
                  pl.BlockSpec((1, s, W_SLOT), lambda qi, bi: (bi, 0, 0)),
                  pl.BlockSpec((1, vt.shape[1], s), lambda qi, bi: (bi, 0, 0)),
                  pl.BlockSpec((H_DIFF, s, tq), lambda qi, bi: (0, 0, qi), pipeline_mode=pl.Buffered(1))],
        out_specs=pl.BlockSpec((1, tq, W_SLOT), lambda qi, bi: (bi, qi, 0)),
        out_shape=jax.ShapeDtypeStruct((b, s, W_SLOT), BF16),
        scratch_shapes=_score_scratch(bounded, s, tq),
        compiler_params=pltpu.CompilerParams(dimension_semantics=("parallel", "parallel"),
                                             vmem_limit_bytes=VMEM_LIMIT),
        name="diff_attn",
    )(lam, q, k, vt, bias_t)


def _na_plan(seq):
    rows = seq // GRID_W
    kr = min(NA_KR_MAX, rows)
    assert rows % NA_QROWS == 0 and rows >= NA_WROWS
    patterns, blocks = [], []
    for r0 in range(0, rows, NA_QROWS):
        ws = int(np.clip(r0 - kr // 2, 0, rows - NA_WROWS))
        rs = [int(np.clip(r0 + j - kr // 2, 0, rows - kr)) for j in range(NA_QROWS)]
        assert all(ws <= x and x + kr <= ws + NA_WROWS for x in rs)
        pat = (r0 - ws, tuple(x - ws for x in rs))
        if pat not in patterns:
            patterns.append(pat)
        blocks.append((r0, ws, patterns.index(pat)))
    return kr, patterns, blocks


def _na_bias_table(rpb, seq):
    kr, patterns, _ = _na_plan(seq)
    nq, nk = NA_QROWS * GRID_W, NA_WROWS * GRID_W
    n_col = 2 * NA_KC - 1
    c = np.arange(GRID_W)
    cstart = np.clip(c - NA_KC // 2, 0, GRID_W - NA_KC)
    col_ok = (c[None, :] >= cstart[:, None]) & (c[None, :] < cstart[:, None] + NA_KC)
    col_idx = np.where(col_ok, c[None, :] - c[:, None] + NA_KC - 1, n_col)
    col_sel = col_idx[:, :, None] == np.arange(n_col + 1)
    j, i = np.arange(NA_QROWS), np.arange(NA_WROWS)
    row_ok, row_sel = [], []
    for d0, rs_rel in patterns:
        rs = np.asarray(rs_rel)
        row_ok.append((i[None, :] >= rs[:, None]) & (i[None, :] < rs[:, None] + kr))
        row_sel.append((i[None, :, None] - (d0 + j[:, None, None]) + NA_KR_MAX - 1) == np.arange(2 * NA_KR_MAX - 1))
    row_ok, row_sel = np.stack(row_ok), np.stack(row_sel)
    rows = jnp.einsum('pjia,hab->hpjib', jnp.asarray(row_sel, F32), rpb * LOG2E, precision=lax.Precision.HIGHEST)
    rows = jnp.concatenate([rows, jnp.full(rows.shape[:-1] + (1,), MASK_VALUE, F32)], axis=-1)
    rows = jnp.where(jnp.asarray(row_ok)[None, :, :, :, None], rows, MASK_VALUE)
    tab = jnp.einsum('hpjib,ckb->hpikjc', rows, jnp.asarray(col_sel, F32), precision=lax.Precision.HIGHEST)
    tab = tab.reshape(H_NA, len(patterns), nk, nq)
    tab = jnp.concatenate([tab, jnp.zeros((2 * N_PAIR - H_NA,) + tab.shape[1:], F32)], axis=0)
    return tab.reshape(N_PAIR, 2, len(patterns), nk, nq)


def _na_attn_kernel(q_ref, k_ref, vt_ref, bias_ref, o_ref, *s_bufs, blocks, vt_rows):
    nq, nk = NA_QROWS * GRID_W, NA_WROWS * GRID_W
    score_fns, vt_fns, places = [], [], []
    for r0, ws, pat in blocks:
        q_rows = slice(r0 * GRID_W, r0 * GRID_W + nq)
        k_rows = slice(ws * GRID_W, ws * GRID_W + nk)
        for p in range(N_PAIR):
            slot = slice(p * LANES, (p + 1) * LANES)

            def score(c, q_rows=q_rows, k_rows=k_rows, slot=slot, p=p, pat=pat):
                q = q_ref[0, q_rows, slot]
                q2 = jnp.concatenate([_lane_band(q, 0, HEAD_DIM), _lane_band(q, HEAD_DIM, HEAD_DIM)], axis=0)
                bias = jnp.concatenate([bias_ref[p, 0, pat], bias_ref[p, 1, pat]], axis=1)
                return _dot_nt(k_ref[0, k_rows, slot], q2) + bias

            score_fns.append(score)
            vt_fns.append(lambda c, k_rows=k_rows, p=p: vt_ref[0, p * vt_rows:(p + 1) * vt_rows, k_rows])
            places.append((q_rows, slot))
    res = _pipelined_softmax_pv(score_fns, vt_fns, s_bufs or None, nk, nk, value_rows=LANES)
    for (acc, l), (q_rows, slot) in zip(res, places):
        o = acc * (1.0 / l)
        pair_t = jnp.concatenate([o[:HEAD_DIM, :nq], o[HEAD_DIM:, nq:]], axis=0)
        o_ref[0, q_rows, slot] = pair_t.T.astype(o_ref.dtype)


def _na_attn(q, k, vt, bias_tab, bounded):
    b, s, _ = q.shape
    _, _, blocks = _na_plan(s)
    nq, nk = NA_QROWS * GRID_W, NA_WROWS * GRID_W
    tok = pl.BlockSpec((1, s, W_SLOT), lambda bi: (bi, 0, 0))
    return pl.pallas_call(
        functools.partial(_na_attn_kernel, blocks=tuple(blocks), vt_rows=LANES if bounded else NA_VT_ROWS),
        grid=(b,),
        in_specs=[tok, tok, pl.BlockSpec((1, vt.shape[1], s), lambda bi: (bi, 0, 0)),
                  _const_spec(bias_tab.shape)],
        out_specs=tok,
        out_shape=jax.ShapeDtypeStruct((b, s, W_SLOT), BF16),
        scratch_shapes=_score_scratch(bounded, nk, 2 * nq),
        compiler_params=pltpu.CompilerParams(dimension_semantics=("parallel",), vmem_limit_bytes=VMEM_LIMIT),
        name="na_attn",
    )(q, k, vt, bias_tab)


def _pad_cols(w, groups):
    parts = []
    for start, width, padded in groups:
        parts.append(w[..., start:start + width])
        if padded > width:
            parts.append(jnp.zeros(w.shape[:-1] + (padded - width,), w.dtype))
    return jnp.concatenate(parts, axis=-1)


def _t5_bucket(rel):
    half = T5_BUCKETS // 2
    max_exact = half // 2
    n = jnp.abs(rel)
    nf = jnp.maximum(n, 1).astype(F32)
    large = max_exact + (jnp.log(nf / max_exact) / math.log(T5_MAX_DIST / max_exact)
                         * (half - max_exact)).astype(jnp.int32)
    large = jnp.minimum(large, half - 1)
    return jnp.where(rel > 0, half, 0) + jnp.where(n < max_exact, n, large)


def _t5_bias_kernel(thr_ref, tab_ref, o_ref):
    half = T5_BUCKETS // 2
    head = pl.program_id(0)
    tk, s = o_ref.shape[1:]
    k = pl.program_id(1) * tk + lax.broadcasted_iota(jnp.int32, (tk, s), 0)
    rel = k - lax.broadcasted_iota(jnp.int32, (tk, s), 1)
    n = jnp.abs(rel)
    neg = jnp.full((tk, s), tab_ref[head * T5_BUCKETS], F32)
    pos = jnp.full((tk, s), tab_ref[head * T5_BUCKETS + half], F32)
    for j in range(1, half):
        ge = n >= thr_ref[j]
        neg = jnp.where(ge, tab_ref[head * T5_BUCKETS + j], neg)
        pos = jnp.where(ge, tab_ref[head * T5_BUCKETS + half + j], pos)
    o_ref[0] = jnp.where(rel > 0, pos, neg)


def _t5_bias(t5_table, seq, tq):
    half = T5_BUCKETS // 2
    dist_bucket = _t5_bucket(-jnp.arange(seq, dtype=jnp.int32))
    thr = jnp.sum(dist_bucket[None, :] < jnp.arange(half, dtype=jnp.int32)[:, None], axis=1).astype(jnp.int32)
    tab = t5_table.T * LOG2E
    return pl.pallas_call(
        _t5_bias_kernel,
        grid=(H_DIFF, seq // tq),
        in_specs=[pl.BlockSpec(memory_space=pltpu.SMEM), pl.BlockSpec(memory_space=pltpu.SMEM)],
        out_specs=pl.BlockSpec((1, tq, seq), lambda h, ki: (h, ki, 0)),
        out_shape=jax.ShapeDtypeStruct((H_DIFF, seq, seq), F32),
        compiler_params=pltpu.CompilerParams(dimension_semantics=("parallel", "parallel"),
                                             vmem_limit_bytes=VMEM_LIMIT),
        name="t5_bias",
    )(thr, tab.reshape(-1))


def _swap_rope_lanes(w):
    r = MLA_ROPE // 2
    return jnp.concatenate([w[..., :MLA_NOPE], w[..., MLA_NOPE + r:], w[..., MLA_NOPE:MLA_NOPE + r]], axis=-1)


def _pad_lanes(w, width=LANES):
    return jnp.concatenate([w, jnp.zeros(w.shape[:-1] + (width - w.shape[-1],), w.dtype)], axis=-1)


def _block_ones(width, seg):
    idx = np.arange(width) // seg
    return jnp.asarray(idx[:, None] == idx[None, :], dtype=BF16)


def kernel(x, ffn1_norm, ffn1_w_gate, ffn1_w_up, ffn1_w_down, mix_norm, w_in, mla_q_norm, mla_w_uq, mla_kv_norm, mla_w_ukv, mla_qk_q_norm, mla_qk_k_norm, mla_out_norm, na_qk_q_norm, na_qk_k_norm, na_rpb, na_out_norm, diff_qk_q_norm, diff_qk_k_norm, diff_lambda_q1, diff_lambda_k1, diff_lambda_q2, diff_lambda_k2, diff_subln, t5_table, w_out, ffn2_norm, ffn2_w_gate, ffn2_w_up, ffn2_w_down):
    b, s, d = x.shape
    depth = w_in.shape[0]
    t = b * s
    tq = min(512, s)

    inv_freq = ROPE_BASE ** (-jnp.arange(0, MLA_ROPE, 2, dtype=F32) / MLA_ROPE)
    ang = jnp.arange(s, dtype=F32)[:, None] * inv_freq[None, :]
    cos, sin = jnp.cos(ang), jnp.sin(ang)
    cos_tab = jnp.concatenate([jnp.ones((s, MLA_NOPE), F32), cos, cos, jnp.zeros((s, LANES - MLA_QK), F32)], axis=-1)
    sin_tab = jnp.concatenate([jnp.zeros((s, MLA_NOPE), F32), -sin, sin, jnp.zeros((s, LANES - MLA_QK), F32)], axis=-1)

    t5_bias = _t5_bias(t5_table, s, tq)
    ones64 = _block_ones(W_SLOT, HEAD_DIM)
    ones32 = _block_ones(W_SLOT, DIFF_D)
    seg = H_NA * HEAD_DIM
    in_groups = [(0, Q_LORA + KV_LORA + MLA_ROPE, 2 * Q_LORA)] + [
        (Q_LORA + KV_LORA + MLA_ROPE + i * seg, seg, W_SLOT) for i in range(6)]

    h = x.reshape(t, d)
    for l in range(depth):
        row = lambda v: v[l][None, :]
        h = _ffn(h, row(ffn1_norm), ffn1_w_gate[l].astype(BF16), ffn1_w_up[l].astype(BF16),
                 ffn1_w_down[l].astype(BF16), s)

        win = _pad_cols(w_in[l], in_groups).astype(BF16)
        wuq = mla_w_uq[l].reshape(Q_LORA, H_MLA, MLA_QK)
        wq_all = jnp.concatenate([_pad_lanes(wuq).reshape(Q_LORA, -1),
                                  _pad_lanes(_swap_rope_lanes(wuq)).reshape(Q_LORA, -1)], axis=-1).astype(BF16)
        wukv = mla_w_ukv[l].reshape(KV_LORA, H_MLA, MLA_NOPE + HEAD_DIM)
        eye = jnp.broadcast_to(jnp.eye(MLA_ROPE, dtype=F32)[:, None, :], (MLA_ROPE, H_MLA, MLA_ROPE))
        k_top = jnp.concatenate([wukv[..., :MLA_NOPE], jnp.zeros((KV_LORA, H_MLA, MLA_ROPE), F32)], axis=-1)
        k_bot = jnp.concatenate([jnp.zeros((MLA_ROPE, H_MLA, MLA_NOPE), F32), eye], axis=-1)
        k_rows = jnp.concatenate([k_top, k_bot, jnp.zeros((KV_LORA - MLA_ROPE, H_MLA, MLA_QK), F32)], axis=0)
        v_rows = jnp.concatenate([wukv[..., MLA_NOPE:].reshape(KV_LORA, -1),
                                  jnp.zeros((KV_LORA, H_MLA * HEAD_DIM), F32)], axis=0)
        wk_all = jnp.concatenate([_pad_lanes(k_rows).reshape(2 * KV_LORA, -1),
                                  _pad_lanes(_swap_rope_lanes(k_rows)).reshape(2 * KV_LORA, -1),
                                  v_rows], axis=-1).astype(BF16)
        gq, gk = mla_qk_q_norm[l], mla_qk_k_norm[l]
        tile_pairs = lambda g, n: _pad_lanes(jnp.tile(g, n), W_SLOT)[None, :]
        consts_a = [row(mix_norm), win, row(mla_q_norm), wq_all, row(mla_kv_norm), wk_all,
                    _pad_lanes(gq)[None, :], _pad_lanes(_swap_rope_lanes(gq))[None, :],
                    _pad_lanes(gk)[None, :], _pad_lanes(_swap_rope_lanes(gk))[None, :]]
        consts_b = [ones64, ones32, tile_pairs(na_qk_q_norm[l], H_NA), tile_pairs(na_qk_k_norm[l], H_NA),
                    tile_pairs(diff_qk_q_norm[l], 2 * H_DIFF), tile_pairs(diff_qk_k_norm[l], 2 * H_DIFF)]
        qm, km, vm, nq, nk, nv, dq, dk, dv = _prep(h, consts_a, cos_tab, sin_tab, consts_b, s)
        seq3 = lambda a: a.reshape(b, s, a.shape[-1])


        def groups_with_ones(vt, n_groups, width):
            vg = vt[:, :n_groups * width].reshape(b, n_groups, width, s)
            vg = jnp.concatenate([vg, jnp.ones((b, n_groups, 16, s), BF16)], axis=2)
            return vg.reshape(b, n_groups * (width + 16), s)

        lam_init = 0.8 - 0.6 * math.exp(-0.3 * l)
        lam = (jnp.exp(jnp.sum(diff_lambda_q1[l] * diff_lambda_k1[l]))
               - jnp.exp(jnp.sum(diff_lambda_q2[l] * diff_lambda_k2[l])) + lam_init).reshape(1)
        na_bias = _na_bias_table(na_rpb[l], s)

        def attention(bounded):
            if bounded:
                vt_na, vt_diff = nv, dv
            else:
                vt_na = groups_with_ones(nv, N_PAIR, LANES)
                vt_diff = groups_with_ones(dv, H_DIFF, HEAD_DIM)
            return (_mla_attn(seq3(qm), seq3(km), vm, tq, bounded),
                    _na_attn(seq3(nq), seq3(nk), vt_na, na_bias, bounded),
                    _diff_attn(lam, seq3(dq), seq3(dk), vt_diff, t5_bias, tq, bounded))

        amax = lambda v: jnp.max(jnp.abs(v))
        logit_bound = LOG2E * jnp.maximum(jnp.maximum(
            math.sqrt(MLA_QK) * amax(mla_qk_q_norm[l]) * amax(mla_qk_k_norm[l]),
            math.sqrt(HEAD_DIM) * amax(na_qk_q_norm[l]) * amax(na_qk_k_norm[l]) + amax(na_rpb[l])),
            math.sqrt(DIFF_D) * amax(diff_qk_q_norm[l]) * amax(diff_qk_k_norm[l]) + amax(t5_table))
        o_mla, o_na, o_diff = lax.cond(logit_bound < BOUNDED_LOGIT_LIMIT,
                                       lambda: attention(True), lambda: attention(False))

        wo = w_out[l]
        wo_all = jnp.concatenate([wo[:H_MLA * HEAD_DIM],
                                  wo[H_MLA * HEAD_DIM:H_MLA * HEAD_DIM + seg], jnp.zeros((W_SLOT - seg, d), F32),
                                  wo[H_MLA * HEAD_DIM + seg:], jnp.zeros((W_SLOT - seg, d), F32)], axis=0).astype(BF16)
        h = _outproj_ffn(h, o_mla.reshape(t, -1), o_na.reshape(t, -1), o_diff.reshape(t, -1),
                         row(mla_out_norm), _pad_lanes(na_out_norm[l], W_SLOT)[None, :],
                         tile_pairs(diff_subln[l], H_DIFF), ones64, wo_all,
                         row(ffn2_norm), ffn2_w_gate[l].astype(BF16), ffn2_w_up[l].astype(BF16),
                         ffn2_w_down[l].astype(BF16), s, 1.0 - lam_init)
    return h.reshape(b, s, d)
```

```python
import functools
import math

import numpy as np
import jax
import jax.numpy as jnp
from jax import lax
from jax.experimental import pallas as pl
from jax.experimental.pallas import tpu as pltpu

HEAD_DIM = 64
H_MLA, H_NA, H_DIFF = 6, 5, 5
Q_LORA, KV_LORA = 256, 128
MLA_NOPE, MLA_ROPE = 64, 32
MLA_QK = MLA_NOPE + MLA_ROPE
ROPE_BASE = 10000.0
GRID_W = 64
NA_KR_MAX, NA_KC = 8, 16
DIFF_D = 32
T5_BUCKETS, T5_MAX_DIST = 32, 128
EPS = 1e-6

LANES = 128
N_PAIR = 3
W_SLOT = N_PAIR * LANES
NA_QROWS = 4
KEY_CHUNK = 1024
VT_ROWS = HEAD_DIM + 16
NA_VT_ROWS = LANES + 16
NA_WROWS = 12
MASK_VALUE = -1e30
LOG2E = math.log2(math.e)
BOUNDED_LOGIT_LIMIT = 60.0
VMEM_LIMIT = 56 * 1024 * 1024

BF16 = jnp.bfloat16
F32 = jnp.float32


def _dot(a, b):
    return jnp.dot(a, b, preferred_element_type=F32)


def _dot_nt(a, b):
    return lax.dot_general(a, b, (((1,), (1,)), ((), ())), preferred_element_type=F32)


def _const_spec(shape):
    nd = len(shape)
    return pl.BlockSpec(shape, lambda *_: (0,) * nd, pipeline_mode=pl.Buffered(1))


def _swiglu_half(h, g_ref, wg_ref, wu_ref, wd_ref):
    ms = jnp.mean(h * h, axis=-1, keepdims=True)
    xn = (h * lax.rsqrt(ms + EPS) * g_ref[...]).astype(BF16)
    gate = _dot(xn, wg_ref[...])
    up = _dot(xn, wu_ref[...])
    act = (gate * (1.0 / (1.0 + jnp.exp(-gate))) * up).astype(BF16)
    return h + 0.5 * _dot(act, wd_ref[...])


def _ffn_kernel(h_ref, g_ref, wg_ref, wu_ref, wd_ref, o_ref):
    o_ref[...] = _swiglu_half(h_ref[...], g_ref, wg_ref, wu_ref, wd_ref)


def _seg_mean_sq(x, ones_ref, seg):
    return _dot((x * x).astype(BF16), ones_ref[...]) * (1.0 / seg)


def _outproj_ffn_kernel(h_ref, om_ref, on_ref, od_ref, gm_ref, gn_ref, gd_ref, ones64_ref, wo_ref,
                        g_ref, wg_ref, wu_ref, wd_ref, o_ref, *, diff_out_scale):
    om = om_ref[...].astype(F32)
    on = on_ref[...].astype(F32)
    od = od_ref[...].astype(F32)
    om = om * lax.rsqrt(jnp.sum(om * om, axis=-1, keepdims=True) * (1.0 / (H_MLA * HEAD_DIM)) + EPS) * gm_ref[...]
    on = on * lax.rsqrt(jnp.sum(on * on, axis=-1, keepdims=True) * (1.0 / (H_NA * HEAD_DIM)) + EPS) * gn_ref[...]
    od = od * lax.rsqrt(_seg_mean_sq(od, ones64_ref, HEAD_DIM) + EPS) * (gd_ref[...] * diff_out_scale)
    cat = jnp.concatenate([om, on, od], axis=-1).astype(BF16)
    h = h_ref[...] + _dot(cat, wo_ref[...])
    o_ref[...] = _swiglu_half(h, g_ref, wg_ref, wu_ref, wd_ref)


def _token_tile(n_tokens, seq):
    tm = min(512, seq)
    assert seq % tm == 0 and n_tokens % tm == 0
    return tm


def _ffn(h, g, wg, wu, wd, seq):
    t, d = h.shape
    tm = _token_tile(t, seq)
    row = pl.BlockSpec((tm, d), lambda i: (i, 0))
    return pl.pallas_call(
        _ffn_kernel,
        grid=(t // tm,),
        in_specs=[row, _const_spec(g.shape), _const_spec(wg.shape), _const_spec(wu.shape), _const_spec(wd.shape)],
        out_specs=row,
        out_shape=jax.ShapeDtypeStruct((t, d), F32),
        compiler_params=pltpu.CompilerParams(dimension_semantics=("parallel",), vmem_limit_bytes=VMEM_LIMIT),
        name="ffn",
    )(h, g, wg, wu, wd)


def _outproj_ffn(h, om, on, od, gm, gn, gd, ones64, wo, g, wg, wu, wd, seq, diff_out_scale):
    t, d = h.shape
    tm = _token_tile(t, seq)
    row = pl.BlockSpec((tm, d), lambda i: (i, 0))
    slot = pl.BlockSpec((tm, W_SLOT), lambda i: (i, 0))
    consts = [gm, gn, gd, ones64, wo, g, wg, wu, wd]
    return pl.pallas_call(
        functools.partial(_outproj_ffn_kernel, diff_out_scale=diff_out_scale),
        grid=(t // tm,),
        in_specs=[row, slot, slot, slot] + [_const_spec(c.shape) for c in consts],
        out_specs=row,
        out_shape=jax.ShapeDtypeStruct((t, d), F32),
        compiler_params=pltpu.CompilerParams(dimension_semantics=("parallel",), vmem_limit_bytes=VMEM_LIMIT),
        name="outproj_ffn",
    )(h, om, on, od, *consts)


def _prep_kernel(h_ref, gmix_ref, win_ref, gq_ref, wq_ref, gkv_ref, wk_ref,
                 gqc_ref, gqs_ref, gkc_ref, gks_ref, cos_ref, sin_ref,
                 ones64_ref, ones32_ref, gnq_ref, gnk_ref, gdq_ref, gdk_ref,
                 qm_ref, km_ref, vm_ref, nq_ref, nk_ref, nv_ref, dq_ref, dk_ref, dv_ref):
    h = h_ref[...]
    ms = jnp.mean(h * h, axis=-1, keepdims=True)
    xn = (h * lax.rsqrt(ms + EPS) * gmix_ref[...]).astype(BF16)
    u = _dot(xn, win_ref[...])

    cq = u[:, 0:Q_LORA]
    cqn = (cq * lax.rsqrt(jnp.mean(cq * cq, axis=-1, keepdims=True) + EPS) * gq_ref[...]).astype(BF16)
    q2 = _dot(cqn, wq_ref[...])
    ckv = u[:, Q_LORA:Q_LORA + KV_LORA]
    ckvn = ckv * lax.rsqrt(jnp.mean(ckv * ckv, axis=-1, keepdims=True) + EPS) * gkv_ref[...]
    kr = u[:, Q_LORA + KV_LORA:Q_LORA + 2 * KV_LORA]
    k2 = _dot(jnp.concatenate([ckvn, kr], axis=-1).astype(BF16), wk_ref[...])

    cos = cos_ref[...]
    sin = sin_ref[...]
    gqc = gqc_ref[...] * cos
    gqs = gqs_ref[...] * sin
    gkc = gkc_ref[...] * cos
    gks = gks_ref[...] * sin
    nsw = H_MLA * LANES
    for hd in range(H_MLA):
        lo, hi = hd * LANES, (hd + 1) * LANES
        qr, qs = q2[:, lo:hi], q2[:, nsw + lo:nsw + hi]
        inv = lax.rsqrt(jnp.sum(qr * qr, axis=-1, keepdims=True) * (1.0 / MLA_QK) + EPS) * (MLA_QK ** -0.5 * LOG2E)
        qm_ref[:, lo:hi] = ((qr * gqc + qs * gqs) * inv).astype(BF16)
        kr_, ks_ = k2[:, lo:hi], k2[:, nsw + lo:nsw + hi]
        inv = lax.rsqrt(jnp.sum(kr_ * kr_, axis=-1, keepdims=True) * (1.0 / MLA_QK) + EPS)
        km_ref[:, lo:hi] = ((kr_ * gkc + ks_ * gks) * inv).astype(BF16)
    vm_ref[0] = k2[:, 2 * nsw:].T.astype(BF16)

    base = 2 * Q_LORA
    naq = u[:, base:base + W_SLOT]
    nak = u[:, base + W_SLOT:base + 2 * W_SLOT]
    nq_ref[...] = (naq * lax.rsqrt(_seg_mean_sq(naq, ones64_ref, HEAD_DIM) + EPS)
                   * (gnq_ref[...] * (HEAD_DIM ** -0.5 * LOG2E))).astype(BF16)
    nk_ref[...] = (nak * lax.rsqrt(_seg_mean_sq(nak, ones64_ref, HEAD_DIM) + EPS) * gnk_ref[...]).astype(BF16)
    nv_ref[0] = u[:, base + 2 * W_SLOT:base + 3 * W_SLOT].T.astype(BF16)
    base += 3 * W_SLOT
    dfq = u[:, base:base + W_SLOT]
    dfk = u[:, base + W_SLOT:base + 2 * W_SLOT]
    dq_ref[...] = (dfq * lax.rsqrt(_seg_mean_sq(dfq, ones32_ref, DIFF_D) + EPS)
                   * (gdq_ref[...] * (DIFF_D ** -0.5 * LOG2E))).astype(BF16)
    dk_ref[...] = (dfk * lax.rsqrt(_seg_mean_sq(dfk, ones32_ref, DIFF_D) + EPS) * gdk_ref[...]).astype(BF16)
    dv_ref[0] = u[:, base + 2 * W_SLOT:base + 3 * W_SLOT].T.astype(BF16)


def _prep(h, consts_a, cos, sin, consts_b, seq):
    t, d = h.shape
    tm = _token_tile(t, seq)
    tiles_per_seq = seq // tm
    row = pl.BlockSpec((tm, d), lambda i: (i, 0))
    pos = pl.BlockSpec((tm, LANES), lambda i: (i % tiles_per_seq, 0))
    widths = [H_MLA * LANES, H_MLA * LANES] + [W_SLOT] * 7
    transposed = [w_i in (2, 5, 8) for w_i in range(len(widths))]
    tok_spec = lambda w: pl.BlockSpec((tm, w), lambda i: (i, 0))
    val_spec = lambda w: pl.BlockSpec((1, w, tm), lambda i: (i // tiles_per_seq, 0, i % tiles_per_seq))
    return pl.pallas_call(
        _prep_kernel,
        grid=(t // tm,),
        in_specs=[row] + [_const_spec(c.shape) for c in consts_a] + [pos, pos]
        + [_const_spec(c.shape) for c in consts_b],
        out_specs=[val_spec(w) if tr else tok_spec(w) for w, tr in zip(widths, transposed)],
        out_shape=[jax.ShapeDtypeStruct((t // seq, w, seq) if tr else (t, w), BF16)
                   for w, tr in zip(widths, transposed)],
        compiler_params=pltpu.CompilerParams(dimension_semantics=("parallel",), vmem_limit_bytes=VMEM_LIMIT),
        name="mix_prep",
    )(h, *consts_a, cos, sin, *consts_b)


def _lane_band(x, lo, width):
    lane = lax.broadcasted_iota(jnp.int32, x.shape, 1)
    return jnp.where((lane >= lo) & (lane < lo + width), x, jnp.zeros_like(x))


def _pipelined_softmax_pv(score_fns, vt_fns, s_bufs, n_keys, chunk, value_rows=HEAD_DIM):
    n_chunk = n_keys // chunk
    rows = lambda c: slice(c * chunk, (c + 1) * chunk)

    def accumulate(u, c, p, acc, l):
        vt = vt_fns[u](c)
        ca = _dot(vt, p.astype(BF16))
        acc = ca if acc is None else acc + ca
        if vt.shape[0] == value_rows:
            cl = jnp.sum(p, axis=0, keepdims=True)
            l = cl if l is None else l + cl
        return acc, l

    finish = lambda acc, l: (acc[:value_rows], acc[value_rows:value_rows + 1] if l is None else l)

    if s_bufs is None:
        results = []
        for u in range(len(score_fns)):
            acc = l = None
            for c in range(n_chunk):
                acc, l = accumulate(u, c, jnp.exp2(score_fns[u](c)), acc, l)
            results.append(finish(acc, l))
        return results

    def score_pass(u, c, m):
        s = score_fns[u](c)
        s_bufs[u % 2][rows(c), :] = s
        cm = jnp.max(s, axis=0, keepdims=True)
        return cm if m is None else jnp.maximum(m, cm)

    m = None
    for c in range(n_chunk):
        m = score_pass(0, c, m)
    results = []
    for u in range(len(score_fns)):
        m_next = acc = l = None
        for c in range(n_chunk):
            if u + 1 < len(score_fns):
                m_next = score_pass(u + 1, c, m_next)
            acc, l = accumulate(u, c, jnp.exp2(s_bufs[u % 2][rows(c), :] - m), acc, l)
        results.append(finish(acc, l))
        m = m_next
    return results


def _store_head_pairs(o_ref, heads_t):
    if len(heads_t) % 2:
        heads_t = heads_t + [jnp.zeros_like(heads_t[0])]
    for p in range(len(heads_t) // 2):
        pair_t = jnp.concatenate(heads_t[2 * p:2 * p + 2], axis=0)
        o_ref[0, :, p * LANES:(p + 1) * LANES] = pair_t.T.astype(o_ref.dtype)


def _mla_attn_kernel(q_ref, k_ref, vt_ref, o_ref, *s_bufs):
    n_keys = k_ref.shape[1]
    chunk = min(KEY_CHUNK, n_keys) if s_bufs else n_keys
    head_cols = lambda h: slice(h * LANES, (h + 1) * LANES)
    key_rows = lambda c: slice(c * chunk, (c + 1) * chunk)
    score_fns = [lambda c, h=h: _dot_nt(k_ref[0, key_rows(c), head_cols(h)], q_ref[0, :, head_cols(h)])
                 for h in range(H_MLA)]
    vt_fns = [lambda c, h=h: vt_ref[0, h * HEAD_DIM:(h + 1) * HEAD_DIM, key_rows(c)] for h in range(H_MLA)]
    res = _pipelined_softmax_pv(score_fns, vt_fns, s_bufs or None, n_keys, chunk)
    _store_head_pairs(o_ref, [acc * (1.0 / l) for acc, l in res])


def _score_scratch(bounded, rows, cols):
    return [] if bounded else [pltpu.VMEM((rows, cols), F32), pltpu.VMEM((rows, cols), F32)]


def _mla_attn(q, k, vt, tq, bounded):
    b, s, _ = q.shape
    return pl.pallas_call(
        _mla_attn_kernel,
        grid=(b, s // tq),
        in_specs=[pl.BlockSpec((1, tq, H_MLA * LANES), lambda bi, qi: (bi, qi, 0)),
                  pl.BlockSpec((1, s, H_MLA * LANES), lambda bi, qi: (bi, 0, 0)),
                  pl.BlockSpec((1, W_SLOT, s), lambda bi, qi: (bi, 0, 0))],
        out_specs=pl.BlockSpec((1, tq, W_SLOT), lambda bi, qi: (bi, qi, 0)),
        out_shape=jax.ShapeDtypeStruct((b, s, W_SLOT), BF16),
        scratch_shapes=_score_scratch(bounded, s, tq),
        compiler_params=pltpu.CompilerParams(dimension_semantics=("parallel", "parallel"),
                                             vmem_limit_bytes=VMEM_LIMIT),
        name="mla_attn",
    )(q, k, vt)


def _diff_attn_kernel(lam_ref, q_ref, k_ref, vt_ref, bias_ref, o_ref, *s_bufs, vt_rows):
    lam = lam_ref[0]
    n_keys = k_ref.shape[1]
    chunk = min(KEY_CHUNK, n_keys) if s_bufs else n_keys
    key_rows = lambda c: slice(c * chunk, (c + 1) * chunk)
    score_fns, vt_fns = [], []
    for h in range(H_DIFF):
        slot = slice((h // 2) * LANES, (h // 2 + 1) * LANES)
        for m in range(2):
            qm = _lane_band(q_ref[0, :, slot], (2 * (h % 2) + m) * DIFF_D, DIFF_D)
            score_fns.append(lambda c, h=h, slot=slot, qm=qm:
                             _dot_nt(k_ref[0, key_rows(c), slot], qm) + bias_ref[h, key_rows(c), :])
            vt_fns.append(lambda c, h=h: vt_ref[0, h * vt_rows:(h + 1) * vt_rows, key_rows(c)])
    res = _pipelined_softmax_pv(score_fns, vt_fns, s_bufs or None, n_keys, chunk)
    heads_t = []
    for h in range(H_DIFF):
        (a1, l1), (a2, l2) = res[2 * h], res[2 * h + 1]
        heads_t.append(a1 * (1.0 / l1) - a2 * (lam / l2))
    _store_head_pairs(o_ref, heads_t)


def _diff_attn(lam, q, k, vt, bias_t, tq, bounded):
    b, s, _ = q.shape
    return pl.pallas_call(
        functools.partial(_diff_attn_kernel, vt_rows=HEAD_DIM if bounded else VT_ROWS),
        grid=(s // tq, b),
        in_specs=[pl.BlockSpec(memory_space=pltpu.SMEM),
                  pl.BlockSpec((1, tq, W_SLOT), lambda qi, bi: (bi, qi, 0)),
                  pl.BlockSpec((1, s, W_SLOT), lambda qi, bi: (bi, 0, 0)),
                  pl.BlockSpec((1, vt.shape[1], s), lambda qi, bi: (bi, 0, 0)),
                  pl.BlockSpec((H_DIFF, s, tq), lambda qi, bi: (0, 0, qi), pipeline_mode=pl.Buffered(1))],
        out_specs=pl.BlockSpec((1, tq, W_SLOT), lambda qi, bi: (bi, qi, 0)),
        out_shape=jax.ShapeDtypeStruct((b, s, W_SLOT), BF16),
        scratch_shapes=_score_scratch(bounded, s, tq),
        compiler_params=pltpu.CompilerParams(dimension_semantics=("parallel", "parallel"),
                                             vmem_limit_bytes=VMEM_LIMIT),
        name="diff_attn",
    )(lam, q, k, vt, bias_t)


def _na_plan(seq):
    rows = seq // GRID_W
    kr = min(NA_KR_MAX, rows)
    assert rows % NA_QROWS == 0 and rows >= NA_WROWS
    patterns, blocks = [], []
    for r0 in range(0, rows, NA_QROWS):
        ws = int(np.clip(r0 - kr // 2, 0, rows - NA_WROWS))
        rs = [int(np.clip(r0 + j - kr // 2, 0, rows - kr)) for j in range(NA_QROWS)]
        assert all(ws <= x and x + kr <= ws + NA_WROWS for x in rs)
        pat = (r0 - ws, tuple(x - ws for x in rs))
        if pat not in patterns:
            patterns.append(pat)
        blocks.append((r0, ws, patterns.index(pat)))
    return kr, patterns, blocks


def _na_bias_table(rpb, seq):
    kr, patterns, _ = _na_plan(seq)
    nq, nk = NA_QROWS * GRID_W, NA_WROWS * GRID_W
    n_col = 2 * NA_KC - 1
    c = np.arange(GRID_W)
    cstart = np.clip(c - NA_KC // 2, 0, GRID_W - NA_KC)
    col_ok = (c[None, :] >= cstart[:, None]) & (c[None, :] < cstart[:, None] + NA_KC)
    col_idx = np.where(col_ok, c[None, :] - c[:, None] + NA_KC - 1, n_col)
    col_sel = col_idx[:, :, None] == np.arange(n_col + 1)
    j, i = np.arange(NA_QROWS), np.arange(NA_WROWS)
    row_ok, row_sel = [], []
    for d0, rs_rel in patterns:
        rs = np.asarray(rs_rel)
        row_ok.append((i[None, :] >= rs[:, None]) & (i[None, :] < rs[:, None] + kr))
        row_sel.append((i[None, :, None] - (d0 + j[:, None, None]) + NA_KR_MAX - 1) == np.arange(2 * NA_KR_MAX - 1))
    row_ok, row_sel = np.stack(row_ok), np.stack(row_sel)
    rows = jnp.einsum('pjia,hab->hpjib', jnp.asarray(row_sel, F32), rpb * LOG2E, precision=lax.Precision.HIGHEST)
    rows = jnp.concatenate([rows, jnp.full(rows.shape[:-1] + (1,), MASK_VALUE, F32)], axis=-1)
    rows = jnp.where(jnp.asarray(row_ok)[None, :, :, :, None], rows, MASK_VALUE)
    tab = jnp.einsum('hpjib,ckb->hpikjc', rows, jnp.asarray(col_sel, F32), precision=lax.Precision.HIGHEST)
    tab = tab.reshape(H_NA, len(patterns), nk, nq)
    tab = jnp.concatenate([tab, jnp.zeros((2 * N_PAIR - H_NA,) + tab.shape[1:], F32)], axis=0)
    return tab.reshape(N_PAIR, 2, len(patterns), nk, nq)


def _na_attn_kernel(q_ref, k_ref, vt_ref, bias_ref, o_ref, *s_bufs, blocks, vt_rows):
    nq, nk = NA_QROWS * GRID_W, NA_WROWS * GRID_W
    score_fns, vt_fns, places = [], [], []
    for r0, ws, pat in blocks:
        q_rows = slice(r0 * GRID_W, r0 * GRID_W + nq)
        k_rows = slice(ws * GRID_W, ws * GRID_W + nk)
        for p in range(N_PAIR):
            slot = slice(p * LANES, (p + 1) * LANES)

            def score(c, q_rows=q_rows, k_rows=k_rows, slot=slot, p=p, pat=pat):
                q = q_ref[0, q_rows, slot]
                q2 = jnp.concatenate([_lane_band(q, 0, HEAD_DIM), _lane_band(q, HEAD_DIM, HEAD_DIM)], axis=0)
                bias = jnp.concatenate([bias_ref[p, 0, pat], bias_ref[p, 1, pat]], axis=1)
                return _dot_nt(k_ref[0, k_rows, slot], q2) + bias

            score_fns.append(score)
            vt_fns.append(lambda c, k_rows=k_rows, p=p: vt_ref[0, p * vt_rows:(p + 1) * vt_rows, k_rows])
            places.append((q_rows, slot))
    res = _pipelined_softmax_pv(score_fns, vt_fns, s_bufs or None, nk, nk, value_rows=LANES)
    for (acc, l), (q_rows, slot) in zip(res, places):
        o = acc * (1.0 / l)
        pair_t = jnp.concatenate([o[:HEAD_DIM, :nq], o[HEAD_DIM:, nq:]], axis=0)
        o_ref[0, q_rows, slot] = pair_t.T.astype(o_ref.dtype)


def _na_attn(q, k, vt, bias_tab, bounded):
    b, s, _ = q.shape
    _, _, blocks = _na_plan(s)
    nq, nk = NA_QROWS * GRID_W, NA_WROWS * GRID_W
    tok = pl.BlockSpec((1, s, W_SLOT), lambda bi: (bi, 0, 0))
    return pl.pallas_call(
        functools.partial(_na_attn_kernel, blocks=tuple(blocks), vt_rows=LANES if bounded else NA_VT_ROWS),
        grid=(b,),
        in_specs=[tok, tok, pl.BlockSpec((1, vt.shape[1], s), lambda bi: (bi, 0, 0)),
                  _const_spec(bias_tab.shape)],
        out_specs=tok,
        out_shape=jax.ShapeDtypeStruct((b, s, W_SLOT), BF16),
        scratch_shapes=_score_scratch(bounded, nk, 2 * nq),
        compiler_params=pltpu.CompilerParams(dimension_semantics=("parallel",), vmem_limit_bytes=VMEM_LIMIT),
        name="na_attn",
    )(q, k, vt, bias_tab)


def _pad_cols(w, groups):
    parts = []
    for start, width, padded in groups:
        parts.append(w[..., start:start + width])
        if padded > width:
            parts.append(jnp.zeros(w.shape[:-1] + (padded - width,), w.dtype))
    return jnp.concatenate(parts, axis=-1)


def _t5_bucket(rel):
    half = T5_BUCKETS // 2
    max_exact = half // 2
    n = jnp.abs(rel)
    nf = jnp.maximum(n, 1).astype(F32)
    large = max_exact + (jnp.log(nf / max_exact) / math.log(T5_MAX_DIST / max_exact)
                         * (half - max_exact)).astype(jnp.int32)
    large = jnp.minimum(large, half - 1)
    return jnp.where(rel > 0, half, 0) + jnp.where(n < max_exact, n, large)


def _t5_bias_kernel(thr_ref, tab_ref, o_ref, m_ref):
    half = T5_BUCKETS // 2
    head = pl.program_id(0)
    s = o_ref.shape[1]
    tk, w = m_ref.shape
    rel = (lax.broadcasted_iota(jnp.int32, (tk, w), 0) - lax.broadcasted_iota(jnp.int32, (tk, w), 1)) + (s - tk)
    n = jnp.abs(rel)
    neg = jnp.full((tk, w), tab_ref[head * T5_BUCKETS], F32)
    pos = jnp.full((tk, w), tab_ref[head * T5_BUCKETS + half], F32)
    for j in range(1, half):
        ge = n >= thr_ref[j]
        neg = jnp.where(ge, tab_ref[head * T5_BUCKETS + j], neg)
        pos = jnp.where(ge, tab_ref[head * T5_BUCKETS + half + j], pos)
    m_ref[...] = jnp.where(rel > 0, pos, neg)
    n_tiles = s // tk
    for i in range(n_tiles):
        start = (n_tiles - 1 - i) * tk
        o_ref[0, i * tk:(i + 1) * tk, :] = m_ref[:, start:start + s]


def _t5_bias(t5_table, seq, tq):
    half = T5_BUCKETS // 2
    dist_bucket = _t5_bucket(-jnp.arange(seq, dtype=jnp.int32))
    thr = jnp.sum(dist_bucket[None, :] < jnp.arange(half, dtype=jnp.int32)[:, None], axis=1).astype(jnp.int32)
    tab = t5_table.T * LOG2E
    return pl.pallas_call(
        _t5_bias_kernel,
        grid=(H_DIFF,),
        in_specs=[pl.BlockSpec(memory_space=pltpu.SMEM), pl.BlockSpec(memory_space=pltpu.SMEM)],
        out_specs=pl.BlockSpec((1, seq, seq), lambda h: (h, 0, 0)),
        out_shape=jax.ShapeDtypeStruct((H_DIFF, seq, seq), F32),
        scratch_shapes=[pltpu.VMEM((tq, 2 * seq - tq), F32)],
        compiler_params=pltpu.CompilerParams(dimension_semantics=("parallel",), vmem_limit_bytes=VMEM_LIMIT),
        name="t5_bias",
    )(thr, tab.reshape(-1))


def _swap_rope_lanes(w):
    r = MLA_ROPE // 2
    return jnp.concatenate([w[..., :MLA_NOPE], w[..., MLA_NOPE + r:], w[..., MLA_NOPE:MLA_NOPE + r]], axis=-1)


def _pad_lanes(w, width=LANES):
    return jnp.concatenate([w, jnp.zeros(w.shape[:-1] + (width - w.shape[-1],), w.dtype)], axis=-1)


def _block_ones(width, seg):
    idx = np.arange(width) // seg
    return jnp.asarray(idx[:, None] == idx[None, :], dtype=BF16)


def kernel(x, ffn1_norm, ffn1_w_gate, ffn1_w_up, ffn1_w_down, mix_norm, w_in, mla_q_norm, mla_w_uq, mla_kv_norm, mla_w_ukv, mla_qk_q_norm, mla_qk_k_norm, mla_out_norm, na_qk_q_norm, na_qk_k_norm, na_rpb, na_out_norm, diff_qk_q_norm, diff_qk_k_norm, diff_lambda_q1, diff_lambda_k1, diff_lambda_q2, diff_lambda_k2, diff_subln, t5_table, w_out, ffn2_norm, ffn2_w_gate, ffn2_w_up, ffn2_w_down):
    b, s, d = x.shape
    depth = w_in.shape[0]
    t = b * s
    tq = min(512, s)

    inv_freq = ROPE_BASE ** (-jnp.arange(0, MLA_ROPE, 2, dtype=F32) / MLA_ROPE)
    ang = jnp.arange(s, dtype=F32)[:, None] * inv_freq[None, :]
    cos, sin = jnp.cos(ang), jnp.sin(ang)
    cos_tab = jnp.concatenate([jnp.ones((s, MLA_NOPE), F32), cos, cos, jnp.zeros((s, LANES - MLA_QK), F32)], axis=-1)
    sin_tab = jnp.concatenate([jnp.zeros((s, MLA_NOPE), F32), -sin, sin, jnp.zeros((s, LANES - MLA_QK), F32)], axis=-1)

    t5_bias = _t5_bias(t5_table, s, tq)
    ones64 = _block_ones(W_SLOT, HEAD_DIM)
    ones32 = _block_ones(W_SLOT, DIFF_D)
    seg = H_NA * HEAD_DIM
    in_groups = [(0, Q_LORA + KV_LORA + MLA_ROPE, 2 * Q_LORA)] + [
        (Q_LORA + KV_LORA + MLA_ROPE + i * seg, seg, W_SLOT) for i in range(6)]

    h = x.reshape(t, d)
    for l in range(depth):
        row = lambda v: v[l][None, :]
        h = _ffn(h, row(ffn1_norm), ffn1_w_gate[l].astype(BF16), ffn1_w_up[l].astype(BF16),
                 ffn1_w_down[l].astype(BF16), s)

        win = _pad_cols(w_in[l], in_groups).astype(BF16)
        wuq = mla_w_uq[l].reshape(Q_LORA, H_MLA, MLA_QK)
        wq_all = jnp.concatenate([_pad_lanes(wuq).reshape(Q_LORA, -1),
                                  _pad_lanes(_swap_rope_lanes(wuq)).reshape(Q_LORA, -1)], axis=-1).astype(BF16)
        wukv = mla_w_ukv[l].reshape(KV_LORA, H_MLA, MLA_NOPE + HEAD_DIM)
        eye = jnp.broadcast_to(jnp.eye(MLA_ROPE, dtype=F32)[:, None, :], (MLA_ROPE, H_MLA, MLA_ROPE))
        k_top = jnp.concatenate([wukv[..., :MLA_NOPE], jnp.zeros((KV_LORA, H_MLA, MLA_ROPE), F32)], axis=-1)
        k_bot = jnp.concatenate([jnp.zeros((MLA_ROPE, H_MLA, MLA_NOPE), F32), eye], axis=-1)
        k_rows = jnp.concatenate([k_top, k_bot, jnp.zeros((KV_LORA - MLA_ROPE, H_MLA, MLA_QK), F32)], axis=0)
        v_rows = jnp.concatenate([wukv[..., MLA_NOPE:].reshape(KV_LORA, -1),
                                  jnp.zeros((KV_LORA, H_MLA * HEAD_DIM), F32)], axis=0)
        wk_all = jnp.concatenate([_pad_lanes(k_rows).reshape(2 * KV_LORA, -1),
                                  _pad_lanes(_swap_rope_lanes(k_rows)).reshape(2 * KV_LORA, -1),
                                  v_rows], axis=-1).astype(BF16)
        gq, gk = mla_qk_q_norm[l], mla_qk_k_norm[l]
        tile_pairs = lambda g, n: _pad_lanes(jnp.tile(g, n), W_SLOT)[None, :]
        consts_a = [row(mix_norm), win, row(mla_q_norm), wq_all, row(mla_kv_norm), wk_all,
                    _pad_lanes(gq)[None, :], _pad_lanes(_swap_rope_lanes(gq))[None, :],
                    _pad_lanes(gk)[None, :], _pad_lanes(_swap_rope_lanes(gk))[None, :]]
        consts_b = [ones64, ones32, tile_pairs(na_qk_q_norm[l], H_NA), tile_pairs(na_qk_k_norm[l], H_NA),
                    tile_pairs(diff_qk_q_norm[l], 2 * H_DIFF), tile_pairs(diff_qk_k_norm[l], 2 * H_DIFF)]
        qm, km, vm, nq, nk, nv, dq, dk, dv = _prep(h, consts_a, cos_tab, sin_tab, consts_b, s)
        seq3 = lambda a: a.reshape(b, s, a.shape[-1])


        def groups_with_ones(vt, n_groups, width):
            vg = vt[:, :n_groups * width].reshape(b, n_groups, width, s)
            vg = jnp.concatenate([vg, jnp.ones((b, n_groups, 16, s), BF16)], axis=2)
            return vg.reshape(b, n_groups * (width + 16), s)

        lam_init = 0.8 - 0.6 * math.exp(-0.3 * l)
        lam = (jnp.exp(jnp.sum(diff_lambda_q1[l] * diff_lambda_k1[l]))
               - jnp.exp(jnp.sum(diff_lambda_q2[l] * diff_lambda_k2[l])) + lam_init).reshape(1)
        na_bias = _na_bias_table(na_rpb[l], s)

        def attention(bounded):
            if bounded:
                vt_na, vt_diff = nv, dv
            else:
                vt_na = groups_with_ones(nv, N_PAIR, LANES)
                vt_diff = groups_with_ones(dv, H_DIFF, HEAD_DIM)
            return (_mla_attn(seq3(qm), seq3(km), vm, min(2 * tq, s) if bounded else tq, bounded),
                    _na_attn(seq3(nq), seq3(nk), vt_na, na_bias, bounded),
                    _diff_attn(lam, seq3(dq), seq3(dk), vt_diff, t5_bias, tq, bounded))

        amax = lambda v: jnp.max(jnp.abs(v))
        logit_bound = LOG2E * jnp.maximum(jnp.maximum(
            math.sqrt(MLA_QK) * amax(mla_qk_q_norm[l]) * amax(mla_qk_k_norm[l]),
            math.sqrt(HEAD_DIM) * amax(na_qk_q_norm[l]) * amax(na_qk_k_norm[l]) + amax(na_rpb[l])),
            math.sqrt(DIFF_D) * amax(diff_qk_q_norm[l]) * amax(diff_qk_k_norm[l]) + amax(t5_table))
        o_mla, o_na, o_diff = lax.cond(logit_bound < BOUNDED_LOGIT_LIMIT,
                                       lambda: attention(True), lambda: attention(False))

        wo = w_out[l]
        wo_all = jnp.concatenate([wo[:H_MLA * HEAD_DIM],
                                  wo[H_MLA * HEAD_DIM:H_MLA * HEAD_DIM + seg], jnp.zeros((W_SLOT - seg, d), F32),
                                  wo[H_MLA * HEAD_DIM + seg:], jnp.zeros((W_SLOT - seg, d), F32)], axis=0).astype(BF16)
        h = _outproj_ffn(h, o_mla.reshape(t, -1), o_na.reshape(t, -1), o_diff.reshape(t, -1),
                         row(mla_out_norm), _pad_lanes(na_out_norm[l], W_SLOT)[None, :],
                         tile_pairs(diff_subln[l], H_DIFF), ones64, wo_all,
                         row(ffn2_norm), ffn2_w_gate[l].astype(BF16), ffn2_w_up[l].astype(BF16),
                         ffn2_w_down[l].astype(BF16), s, 1.0 - lam_init)
    return h.reshape(b, s, d)
```
